```python
import math
import jax
import jax.numpy as jnp
from jax import lax
import numpy as np

D_MODEL = 2048
BATCH = 1
SEQ = 16384
DEPTH = 2
DEC_BATCH = 16
DEC_SEQ = 32
PAST_LEN = 1024

CHUNK = 64
Q_BLOCK = 128
D_MIX = D_MODEL
W_A = D_MIX // 2
W_B = D_MIX // 4
W_C = D_MIX - W_A - W_B
A_HEAD_DIM = 64
A_VDIM = 2 * A_HEAD_DIM
A_HEADS = W_A // A_VDIM
B_HEAD_DIM = 128
B_HEADS = W_B // B_HEAD_DIM
C_GROUP = 16
C_GROUPS = W_C // C_GROUP
C_STATE = 64
D_FF = 5504
N_EXPERTS = 8
TOP_K = 2
D_FF_EXPERT = 5504
N_DENSE = (DEPTH + 1) // 2
N_MOE = DEPTH // 2
IN_SPLITS = (W_A, 2 * W_A, 3 * W_A, 3 * W_A + W_B, 3 * W_A + 2 * W_B, 3 * W_A + 3 * W_B, 3 * W_A + 4 * W_B)
IN_COLS = 3 * W_A + 4 * W_B + W_C
RMS_EPS = 1e-6
NEG_INF = -1e30
A_RE_MAX = -1e-4

kernel_name = 'hymba_diffattn_hgrn2_s5_streaming_step'


def rms_norm(x, g):
    xf = x.astype(jnp.float32)
    y = xf * lax.rsqrt(jnp.mean(xf * xf, axis=-1, keepdims=True) + RMS_EPS)
    return (y * g.astype(jnp.float32)).astype(x.dtype)


def swiglu(h, w1, w3, w2):
    return (jax.nn.silu(h @ w1) * (h @ w3)) @ w2


def moe_ffn(h, w_router, w1, w3, w2):
    logits = (h @ w_router).astype(jnp.float32)
    top_v, top_i = lax.top_k(logits, TOP_K)
    top_w = jax.nn.softmax(top_v, axis=-1)
    gates = jnp.sum(jax.nn.one_hot(top_i, N_EXPERTS, dtype=jnp.float32) * top_w[..., None], axis=-2)
    y = gates[..., 0:1] * swiglu(h, w1[0], w3[0], w2[0])
    for e in range(1, N_EXPERTS):
        y = y + gates[..., e:e + 1] * swiglu(h, w1[e], w3[e], w2[e])
    return y.astype(h.dtype)


def diff_attn_core(q, k, v, q_pos, k_pos, lam):
    s = jnp.einsum('bqhcd,bkhcd->bhcqk', q.astype(jnp.float32), k.astype(jnp.float32)) * (A_HEAD_DIM ** -0.5)
    visible = (k_pos[None, :] // CHUNK) <= (q_pos[:, None] // CHUNK)
    p = jax.nn.softmax(jnp.where(visible, s, NEG_INF), axis=-1)
    w = p[:, :, 0] - lam * p[:, :, 1]
    return jnp.einsum('bhqk,bkhe->bqhe', w, v.astype(jnp.float32))


def diff_attn_prompt(q, k, v, lam):
    bsz, t = q.shape[0], q.shape[1]
    n_blk = t // Q_BLOCK
    pos = jnp.arange(t)
    q_blk = jnp.swapaxes(q.reshape(bsz, n_blk, Q_BLOCK, A_HEADS, 2, A_HEAD_DIM), 0, 1)
    pos_blk = pos.reshape(n_blk, Q_BLOCK)
    out = lax.map(lambda a: diff_attn_core(a[0], k, v, a[1], pos, lam), (q_blk, pos_blk))
    return jnp.swapaxes(out, 0, 1).reshape(bsz, t, A_HEADS, A_VDIM)


def hgrn2_chunkwise(q, log_f, k, v, s0, chunk):
    bsz, t, nh, dk = q.shape
    n = t // chunk

    def to_chunks(a):
        return jnp.swapaxes(a.reshape(bsz, n, chunk, nh, a.shape[-1]), 0, 1)

    causal = jnp.tril(jnp.ones((chunk, chunk), dtype=bool))[None, :, :, None, None]

    def step(s, xs):
        qc, lfc, kc, vc = xs
        b = jnp.cumsum(lfc, axis=1)
        diff = b[:, :, None] - b[:, None, :]
        decay = jnp.exp(jnp.where(causal, diff, -jnp.inf))
        att = jnp.einsum('bthd,btshd,bshd->bhts', qc, decay, kc)
        o = (jnp.einsum('bhts,bshe->bthe', att, vc)
             + jnp.einsum('bthd,bhde->bthe', qc * jnp.exp(b), s))
        b_last = b[:, -1]
        s_new = (jnp.exp(b_last)[..., None] * s
                 + jnp.einsum('bshd,bshe->bhde', kc * jnp.exp(b_last[:, None] - b), vc))
        return s_new, o

    s_fin, o = lax.scan(step, s0, (to_chunks(q), to_chunks(log_f), to_chunks(k), to_chunks(v)))
    return jnp.swapaxes(o, 0, 1).reshape(bsz, t, nh, v.shape[-1]), s_fin


def s5_discretize(a_re, a_im, b_re, b_im, log_dt):
    lam_re = jnp.minimum(a_re.astype(jnp.float32), A_RE_MAX)
    lam_im = a_im.astype(jnp.float32)
    dt = jnp.exp(log_dt.astype(jnp.float32))[:, None]
    mag = jnp.exp(lam_re * dt)
    abar_re = mag * jnp.cos(lam_im * dt)
    abar_im = mag * jnp.sin(lam_im * dt)
    den = lam_re * lam_re + lam_im * lam_im
    z_re = abar_re - 1.0
    coef_re = (z_re * lam_re + abar_im * lam_im) / den
    coef_im = (abar_im * lam_re - z_re * lam_im) / den
    b_re = b_re.astype(jnp.float32)
    b_im = b_im.astype(jnp.float32)
    bbar_re = coef_re[..., None] * b_re - coef_im[..., None] * b_im
    bbar_im = coef_re[..., None] * b_im + coef_im[..., None] * b_re
    return abar_re, abar_im, bbar_re, bbar_im


def s5_scan(u, x0_re, x0_im, abar_re, abar_im, bbar_re, bbar_im):
    bu_re = jnp.einsum('btgp,gnp->btgn', u, bbar_re)
    bu_im = jnp.einsum('btgp,gnp->btgn', u, bbar_im)
    bu_re = bu_re.at[:, 0].add(abar_re * x0_re - abar_im * x0_im)
    bu_im = bu_im.at[:, 0].add(abar_re * x0_im + abar_im * x0_re)
    a_re = jnp.broadcast_to(abar_re, bu_re.shape)
    a_im = jnp.broadcast_to(abar_im, bu_im.shape)

    def combine(e1, e2):
        a1r, a1i, b1r, b1i = e1
        a2r, a2i, b2r, b2i = e2
        return (a2r * a1r - a2i * a1i, a2r * a1i + a2i * a1r,
                a2r * b1r - a2i * b1i + b2r, a2r * b1i + a2i * b1r + b2i)

    _, _, x_re, x_im = lax.associative_scan(combine, (a_re, a_im, bu_re, bu_im), axis=1)
    return x_re, x_im


def setup_inputs(seed: int = 0) -> dict:
    key = jax.random.key(seed)
    keys = iter(jax.random.split(key, 48))
    f32 = jnp.float32

    def nrm(shape, scale):
        return scale * jax.random.normal(next(keys), shape, f32)

    def gain(shape):
        return 1.0 + nrm(shape, 0.02)

    a_im0 = math.pi * jnp.arange(C_STATE, dtype=f32)
    return {
        'x_prompt': nrm((BATCH, SEQ, D_MODEL), 1.0),
        'x_sample': nrm((DEC_BATCH, DEC_SEQ, D_MODEL), 1.0),
        'c_prompt': nrm((BATCH, D_MODEL), 1.0),
        'c_sample': nrm((DEC_BATCH, D_MODEL), 1.0),
        'cache_k': nrm((DEPTH, DEC_BATCH, PAST_LEN, A_HEADS, 2, A_HEAD_DIM), 1.0),
        'cache_v': nrm((DEPTH, DEC_BATCH, PAST_LEN, A_HEADS, A_VDIM), 1.0),
        'state_hgrn': nrm((DEPTH, DEC_BATCH, B_HEADS, B_HEAD_DIM, B_HEAD_DIM), 0.3),
        'state_ssm_re': nrm((DEPTH, DEC_BATCH, C_GROUPS, C_STATE), 0.3),
        'state_ssm_im': nrm((DEPTH, DEC_BATCH, C_GROUPS, C_STATE), 0.3),
        'w_ada': nrm((DEPTH, D_MODEL, 6 * D_MODEL), 0.5 * D_MODEL ** -0.5),
        'b_ada': nrm((DEPTH, 6 * D_MODEL), 0.01),
        'norm1_g': gain((DEPTH, D_MODEL)),
        'norm2_g': gain((DEPTH, D_MODEL)),
        'w_in': nrm((DEPTH, D_MODEL, IN_COLS), D_MODEL ** -0.5),
        'w_out': nrm((DEPTH, D_MIX, D_MODEL), D_MIX ** -0.5),
        'q_norm_g': gain((DEPTH, A_HEAD_DIM)),
        'k_norm_g': gain((DEPTH, A_HEAD_DIM)),
        'lambda_qk': nrm((DEPTH, 4, A_HEAD_DIM), 0.1),
        'attn_out_g': gain((DEPTH, A_VDIM)),
        'hgrn_lower_bounds': nrm((DEPTH, W_B), 0.1),
        'hgrn_out_g': gain((DEPTH, B_HEAD_DIM)),
        'ssm_a_re': -0.5 + nrm((DEPTH, C_GROUPS, C_STATE), 0.01),
        'ssm_a_im': a_im0 + nrm((DEPTH, C_GROUPS, C_STATE), 0.01),
        'ssm_b_re': nrm((DEPTH, C_GROUPS, C_STATE, C_GROUP), (2 * C_GROUP) ** -0.5),
        'ssm_b_im': nrm((DEPTH, C_GROUPS, C_STATE, C_GROUP), (2 * C_GROUP) ** -0.5),
        'ssm_c_re': nrm((DEPTH, C_GROUPS, C_GROUP, C_STATE), (2 * C_STATE) ** -0.5),
        'ssm_c_im': nrm((DEPTH, C_GROUPS, C_GROUP, C_STATE), (2 * C_STATE) ** -0.5),
        'ssm_d': nrm((DEPTH, C_GROUPS, C_GROUP), 1.0),
        'ssm_log_dt': jax.random.uniform(next(keys), (DEPTH, C_GROUPS), f32, math.log(1e-3), math.log(1e-1)),
        'ssm_glu_w': nrm((DEPTH, W_C, 2 * W_C), W_C ** -0.5),
        'ssm_out_g': gain((DEPTH, W_C)),
        'ffn_w1': nrm((N_DENSE, D_MODEL, D_FF), D_MODEL ** -0.5),
        'ffn_w3': nrm((N_DENSE, D_MODEL, D_FF), D_MODEL ** -0.5),
        'ffn_w2': nrm((N_DENSE, D_FF, D_MODEL), D_FF ** -0.5),
        'moe_router': nrm((N_MOE, D_MODEL, N_EXPERTS), D_MODEL ** -0.5),
        'moe_w1': nrm((N_MOE, N_EXPERTS, D_MODEL, D_FF_EXPERT), D_MODEL ** -0.5),
        'moe_w3': nrm((N_MOE, N_EXPERTS, D_MODEL, D_FF_EXPERT), D_MODEL ** -0.5),
        'moe_w2': nrm((N_MOE, N_EXPERTS, D_FF_EXPERT, D_MODEL), D_FF_EXPERT ** -0.5),
    }


def reference(x_prompt, x_sample, c_prompt, c_sample, cache_k, cache_v, state_hgrn,
              state_ssm_re, state_ssm_im, w_ada, b_ada, norm1_g, norm2_g, w_in, w_out,
              q_norm_g, k_norm_g, lambda_qk, attn_out_g, hgrn_lower_bounds, hgrn_out_g,
              ssm_a_re, ssm_a_im, ssm_b_re, ssm_b_im, ssm_c_re, ssm_c_im, ssm_d,
              ssm_log_dt, ssm_glu_w, ssm_out_g, ffn_w1, ffn_w3, ffn_w2,
              moe_router, moe_w1, moe_w3, moe_w2):
    f32 = jnp.float32
    lb_p = jax.nn.softmax(hgrn_lower_bounds.astype(f32), axis=0)
    lower_bounds = jnp.cumsum(lb_p, axis=0) - lb_p[0]

    def token_mixers(h, l, past_k, past_v, s_hgrn0, ssm0_re, ssm0_im):
        bsz, t, _ = h.shape
        qa, ka, va, qb, fb, ib, gb, uc = jnp.split(h @ w_in[l], IN_SPLITS, axis=-1)
        qa = rms_norm(qa.reshape(bsz, t, A_HEADS, 2, A_HEAD_DIM), q_norm_g[l])
        ka = rms_norm(ka.reshape(bsz, t, A_HEADS, 2, A_HEAD_DIM), k_norm_g[l])
        va = va.reshape(bsz, t, A_HEADS, A_VDIM)
        lam_init = 0.8 - 0.6 * math.exp(-0.3 * l)
        lq = lambda_qk[l].astype(f32)
        lam = jnp.exp(jnp.sum(lq[0] * lq[1])) - jnp.exp(jnp.sum(lq[2] * lq[3])) + lam_init
        if past_k is None:
            oa = diff_attn_prompt(qa, ka, va, lam)
        else:
            n_past = past_k.shape[1]
            k_all = jnp.concatenate([past_k.astype(ka.dtype), ka], axis=1)
            v_all = jnp.concatenate([past_v.astype(va.dtype), va], axis=1)
            k_pos = jnp.arange(n_past + t)
            oa = diff_attn_core(qa, k_all, v_all, k_pos[n_past:], k_pos, lam)
        oa = (rms_norm(oa, attn_out_g[l]) * (1.0 - lam_init)).reshape(bsz, t, W_A)
        hs = (bsz, t, B_HEADS, B_HEAD_DIM)
        lb = lower_bounds[l]
        fgate = lb + (1.0 - lb) * jax.nn.sigmoid(fb.astype(f32))
        ob, s_hgrn = hgrn2_chunkwise(
            jax.nn.silu(qb.astype(f32)).reshape(hs), jnp.log(fgate).reshape(hs),
            (1.0 - fgate).reshape(hs), ib.astype(f32).reshape(hs), s_hgrn0.astype(f32),
            min(t, CHUNK))
        ob = (rms_norm(ob, hgrn_out_g[l]) * jax.nn.silu(gb.astype(f32).reshape(hs))).reshape(bsz, t, W_B)
        abar_re, abar_im, bbar_re, bbar_im = s5_discretize(
            ssm_a_re[l], ssm_a_im[l], ssm_b_re[l], ssm_b_im[l], ssm_log_dt[l])
        u = uc.astype(f32).reshape(bsz, t, C_GROUPS, C_GROUP)
        xs_re, xs_im = s5_scan(u, ssm0_re.astype(f32), ssm0_im.astype(f32),
                               abar_re, abar_im, bbar_re, bbar_im)
        yc = (jnp.einsum('btgn,gpn->btgp', xs_re, ssm_c_re[l].astype(f32))
              - jnp.einsum('btgn,gpn->btgp', xs_im, ssm_c_im[l].astype(f32))
              + ssm_d[l].astype(f32) * u)
        yc = jax.nn.gelu(yc.reshape(bsz, t, W_C))
        ga, gg = jnp.split(yc @ ssm_glu_w[l].astype(f32), 2, axis=-1)
        oc = rms_norm(ga * jax.nn.sigmoid(gg), ssm_out_g[l])
        merged = jnp.concatenate([oa, ob, oc], axis=-1).astype(h.dtype)
        return merged @ w_out[l], ka, va, s_hgrn, xs_re[:, -1], xs_im[:, -1]

    def layer(x, c, l, past_k, past_v, s_hgrn0, ssm0_re, ssm0_im):
        mod = jax.nn.silu(c) @ w_ada[l] + b_ada[l]
        sh1, sc1, g1, sh2, sc2, g2 = jnp.split(mod[:, None, :], 6, axis=-1)
        h = rms_norm(x, norm1_g[l]) * (1.0 + sc1) + sh1
        mix, k_new, v_new, s_new, re_new, im_new = token_mixers(
            h, l, past_k, past_v, s_hgrn0, ssm0_re, ssm0_im)
        x = x + g1 * mix
        h = rms_norm(x, norm2_g[l]) * (1.0 + sc2) + sh2
        if l % 2 == 0:
            f = swiglu(h, ffn_w1[l // 2], ffn_w3[l // 2], ffn_w2[l // 2])
        else:
            f = moe_ffn(h, moe_router[l // 2], moe_w1[l // 2], moe_w3[l // 2], moe_w2[l // 2])
        return x + g2 * f, k_new, v_new, s_new, re_new, im_new

    bp = x_prompt.shape[0]
    zero_hgrn = jnp.zeros((bp, B_HEADS, B_HEAD_DIM, B_HEAD_DIM), f32)
    zero_ssm = jnp.zeros((bp, C_GROUPS, C_STATE), f32)
    y_p, y_s = x_prompt, x_sample
    kp, vp, sp, rep, imp = [], [], [], [], []
    ks, vs, ss, res, ims = [], [], [], [], []
    for l in range(DEPTH):
        y_p, k1, v1, s1, r1, i1 = layer(y_p, c_prompt, l, None, None, zero_hgrn, zero_ssm, zero_ssm)
        kp.append(k1)
        vp.append(v1)
        sp.append(s1)
        rep.append(r1)
        imp.append(i1)
        y_s, k2, v2, s2, r2, i2 = layer(y_s, c_sample, l, cache_k[l], cache_v[l],
                                        state_hgrn[l], state_ssm_re[l], state_ssm_im[l])
        ks.append(k2)
        vs.append(v2)
        ss.append(s2)
        res.append(r2)
        ims.append(i2)
    return (y_p, y_s,
            jnp.stack(kp), jnp.stack(vp), jnp.stack(sp), jnp.stack(rep), jnp.stack(imp),
            jnp.stack(ks), jnp.stack(vs), jnp.stack(ss), jnp.stack(res), jnp.stack(ims))
```

```python
import functools
import math

import jax
import jax.numpy as jnp
from jax import lax
from jax.experimental import pallas as pl
from jax.experimental.pallas import tpu as pltpu

F32 = jnp.float32
BF16 = jnp.bfloat16
HIGHEST = lax.Precision.HIGHEST

CHUNK = 64
A_HEAD_DIM = 64
A_VDIM = 2 * A_HEAD_DIM
B_HEAD_DIM = 128
C_GROUP = 16
C_STATE = 64
TOP_K = 2
RMS_EPS = 1e-6
NEG_INF = -1e30
A_RE_MAX = -1e-4

LANES = 128
SUBLANES = 8
ROW_TILE = 512
MOE_ROW_TILE = 256
ATTN_TILE = 512
S5_TILE = 512
VMEM_LIMIT = 48 * 1024 * 1024

NT_DIMS = (((1,), (1,)), ((), ()))
TN_DIMS = (((0,), (0,)), ((), ()))
NN_DIMS = (((1,), (0,)), ((), ()))


def _cparams(n_axes):
    return pltpu.CompilerParams(
        dimension_semantics=("arbitrary",) * n_axes, vmem_limit_bytes=VMEM_LIMIT)


def _round_up(x, m):
    return (x + m - 1) // m * m


def _pick_tile(n, target):
    best = LANES
    for t in range(LANES, min(n, target) + 1, LANES):
        if n % t == 0:
            best = t
    return best


def _split(x):
    hi = x.astype(BF16)
    lo = (x - hi.astype(F32)).astype(BF16)
    return hi, lo


def _dot3(a, b, dims=NN_DIMS):
    ah, al = _split(a)
    bh, bl = _split(b)
    d = functools.partial(lax.dot_general, dimension_numbers=dims, preferred_element_type=F32)
    return d(ah, bh) + d(al, bh) + d(ah, bl)


def _dot_exact(a, b):
    return jnp.dot(a, b, precision=HIGHEST, preferred_element_type=F32)


def _sigmoid(x):
    return 1.0 / (1.0 + jnp.exp(-x))


def _silu(x):
    return x * _sigmoid(x)


def _rms(x):
    return x * lax.rsqrt(jnp.mean(x * x, axis=-1, keepdims=True) + RMS_EPS)


def _mod_kernel(c_ref, w_ref, b_ref, o_ref):
    o_ref[0] = _dot3(_silu(c_ref[...]), w_ref[0]) + b_ref[0]


def _modulation(c_rows, w_ada, b_ada):
    depth, d, n = w_ada.shape
    rows = c_rows.shape[0]
    tn = _pick_tile(n, 1024)
    return pl.pallas_call(
        _mod_kernel,
        out_shape=jax.ShapeDtypeStruct((depth, rows, n), F32),
        grid=(depth, n // tn),
        in_specs=[
            pl.BlockSpec((rows, d), lambda l, j: (0, 0)),
            pl.BlockSpec((1, d, tn), lambda l, j: (l, 0, j)),
            pl.BlockSpec((1, 1, tn), lambda l, j: (l, 0, j)),
        ],
        out_specs=pl.BlockSpec((1, rows, tn), lambda l, j: (l, 0, j)),
        compiler_params=_cparams(2),
        name="adaln_mod",
    )(c_rows, w_ada, b_ada.reshape(depth, 1, n))


def _norm_mod_kernel(x_ref, g_ref, sc_ref, sh_ref, h_ref):
    y = _rms(x_ref[...]) * g_ref[...]
    h_ref[...] = (y * (1.0 + sc_ref[...]) + sh_ref[...]).astype(h_ref.dtype)


def _norm_mod_route_kernel(x_ref, g_ref, sc_ref, sh_ref, wr_ref, h_ref, r_ref, *, n_experts):
    y = _rms(x_ref[...]) * g_ref[...]
    h = y * (1.0 + sc_ref[...]) + sh_ref[...]
    h_ref[...] = h.astype(h_ref.dtype)
    logits = _dot3(h, wr_ref[...])
    lane = lax.broadcasted_iota(jnp.int32, logits.shape, 1).astype(F32)
    lg = jnp.where(lane < n_experts, logits, -jnp.inf)
    m1 = jnp.max(lg, axis=1, keepdims=True)
    i1 = jnp.min(jnp.where(lg == m1, lane, float(LANES)), axis=1, keepdims=True)
    lg2 = jnp.where(lane == i1, -jnp.inf, lg)
    m2 = jnp.max(lg2, axis=1, keepdims=True)
    i2 = jnp.min(jnp.where(lg2 == m2, lane, float(LANES)), axis=1, keepdims=True)
    e = jnp.exp(m2 - m1)
    inv = 1.0 / (1.0 + e)
    r_ref[...] = jnp.where(lane == 0.0, i1, jnp.where(lane == 1.0, i2, jnp.where(
        lane == 2.0, inv, jnp.where(lane == 3.0, e * inv, 0.0))))


def _mod_index(n_prompt_tiles):
    return lambda i: (jnp.maximum(i - n_prompt_tiles + 1, 0), 0)


def _norm_mod(x, g, sc, sh, n_prompt_tiles, w_router=None):
    m, d = x.shape
    mi = _mod_index(n_prompt_tiles)
    in_specs = [
        pl.BlockSpec((ROW_TILE, d), lambda i: (i, 0)),
        pl.BlockSpec((1, d), lambda i: (0, 0)),
        pl.BlockSpec((ROW_TILE, d), mi),
        pl.BlockSpec((ROW_TILE, d), mi),
    ]
    h_spec = pl.BlockSpec((ROW_TILE, d), lambda i: (i, 0))
    if w_router is None:
        return pl.pallas_call(
            _norm_mod_kernel,
            out_shape=jax.ShapeDtypeStruct((m, d), BF16),
            grid=(m // ROW_TILE,),
            in_specs=in_specs,
            out_specs=h_spec,
            compiler_params=_cparams(1),
            name="norm_mod",
        )(x, g, sc, sh)
    n_experts = w_router.shape[1]
    wr = jnp.zeros((d, LANES), F32).at[:, :n_experts].set(w_router.astype(F32))
    return pl.pallas_call(
        functools.partial(_norm_mod_route_kernel, n_experts=n_experts),
        out_shape=(jax.ShapeDtypeStruct((m, d), BF16), jax.ShapeDtypeStruct((m, LANES), F32)),
        grid=(m // ROW_TILE,),
        in_specs=in_specs + [pl.BlockSpec((d, LANES), lambda i: (0, 0))],
        out_specs=(h_spec, pl.BlockSpec((ROW_TILE, LANES), lambda i: (i, 0))),
        compiler_params=_cparams(1),
        name="norm_mod_route",
    )(x, g, sc, sh, wr)


def _mm_kernel(a_ref, w_ref, o_ref):
    o_ref[...] = jnp.dot(a_ref[...], w_ref[...], preferred_element_type=F32).astype(o_ref.dtype)


def _mm_res_kernel(a_ref, w_ref, x_ref, g_ref, o_ref):
    acc = jnp.dot(a_ref[...], w_ref[...], preferred_element_type=F32)
    o_ref[...] = x_ref[...] + g_ref[...] * acc


def _swiglu_up_kernel(a_ref, w1_ref, w3_ref, o_ref):
    a = a_ref[...]
    u = jnp.dot(a, w1_ref[...], preferred_element_type=F32)
    v = jnp.dot(a, w3_ref[...], preferred_element_type=F32)
    o_ref[...] = (_silu(u) * v).astype(o_ref.dtype)


def _matmul(a, w, tn_target=1408):
    m, k = a.shape
    n = w.shape[1]
    tn = _pick_tile(n, tn_target)
    return pl.pallas_call(
        _mm_kernel,
        out_shape=jax.ShapeDtypeStruct((m, n), F32),
        grid=(n // tn, m // ROW_TILE),
        in_specs=[
            pl.BlockSpec((ROW_TILE, k), lambda j, i: (i, 0)),
            pl.BlockSpec((k, tn), lambda j, i: (0, j)),
        ],
        out_specs=pl.BlockSpec((ROW_TILE, tn), lambda j, i: (i, j)),
        compiler_params=_cparams(2),
        name="matmul",
    )(a, w)


def _matmul_residual(a, w, x, gate, n_prompt_tiles, tn_target=512):
    m, k = a.shape
    n = w.shape[1]
    tn = _pick_tile(n, tn_target)
    return pl.pallas_call(
        _mm_res_kernel,
        out_shape=jax.ShapeDtypeStruct((m, n), F32),
        grid=(n // tn, m // ROW_TILE),
        in_specs=[
            pl.BlockSpec((ROW_TILE, k), lambda j, i: (i, 0)),
            pl.BlockSpec((k, tn), lambda j, i: (0, j)),
            pl.BlockSpec((ROW_TILE, tn), lambda j, i: (i, j)),
            pl.BlockSpec((ROW_TILE, tn), lambda j, i: (jnp.maximum(i - n_prompt_tiles + 1, 0), j)),
        ],
        out_specs=pl.BlockSpec((ROW_TILE, tn), lambda j, i: (i, j)),
        compiler_params=_cparams(2),
        name="matmul_residual",
    )(a, w, x, gate)


def _swiglu_up(a, w1, w3, tn_target=1408):
    m, k = a.shape
    n = w1.shape[1]
    tn = _pick_tile(n, tn_target)
    return pl.pallas_call(
        _swiglu_up_kernel,
        out_shape=jax.ShapeDtypeStruct((m, n), BF16),
        grid=(n // tn, m // ROW_TILE),
        in_specs=[
            pl.BlockSpec((ROW_TILE, k), lambda j, i: (i, 0)),
            pl.BlockSpec((k, tn), lambda j, i: (0, j)),
            pl.BlockSpec((k, tn), lambda j, i: (0, j)),
        ],
        out_specs=pl.BlockSpec((ROW_TILE, tn), lambda j, i: (i, j)),
        compiler_params=_cparams(2),
        name="swiglu_up",
    )(a, w1, w3)


def _g_up_kernel(te_ref, nv_ref, a_ref, w1_ref, w3_ref, o_ref):
    i = pl.program_id(1)

    @pl.when(i < nv_ref[0])
    def _():
        a = a_ref[...]
        u = jnp.dot(a, w1_ref[0], preferred_element_type=F32)
        v = jnp.dot(a, w3_ref[0], preferred_element_type=F32)
        o_ref[...] = (_silu(u) * v).astype(o_ref.dtype)

    @pl.when(i >= nv_ref[0])
    def _():
        o_ref[...] = jnp.zeros_like(o_ref)


def _g_down_kernel(te_ref, nv_ref, a_ref, w_ref, o_ref):
    i = pl.program_id(1)

    @pl.when(i < nv_ref[0])
    def _():
        o_ref[...] = jnp.dot(a_ref[...], w_ref[0], preferred_element_type=F32)

    @pl.when(i >= nv_ref[0])
    def _():
        o_ref[...] = jnp.zeros_like(o_ref)


def _grouped_swiglu_up(a, w1, w3, tile_expert, n_valid, tn_target=1408):
    m, k = a.shape
    n = w1.shape[2]
    tn = _pick_tile(n, tn_target)
    tm = MOE_ROW_TILE
    grid_spec = pltpu.PrefetchScalarGridSpec(
        num_scalar_prefetch=2,
        grid=(n // tn, m // tm),
        in_specs=[
            pl.BlockSpec((tm, k), lambda j, i, te, nv: (i, 0)),
            pl.BlockSpec((1, k, tn), lambda j, i, te, nv: (te[i], 0, j)),
            pl.BlockSpec((1, k, tn), lambda j, i, te, nv: (te[i], 0, j)),
        ],
        out_specs=pl.BlockSpec((tm, tn), lambda j, i, te, nv: (i, j)),
    )
    return pl.pallas_call(
        _g_up_kernel,
        out_shape=jax.ShapeDtypeStruct((m, n), BF16),
        grid_spec=grid_spec,
        compiler_params=_cparams(2),
        name="moe_swiglu_up",
    )(tile_expert, n_valid, a, w1, w3)


def _grouped_down(a, w, tile_expert, n_valid, tn_target=512):
    m, k = a.shape
    n = w.shape[2]
    tn = _pick_tile(n, tn_target)
    tm = MOE_ROW_TILE
    grid_spec = pltpu.PrefetchScalarGridSpec(
        num_scalar_prefetch=2,
        grid=(n // tn, m // tm),
        in_specs=[
            pl.BlockSpec((tm, k), lambda j, i, te, nv: (i, 0)),
            pl.BlockSpec((1, k, tn), lambda j, i, te, nv: (te[i], 0, j)),
        ],
        out_specs=pl.BlockSpec((tm, tn), lambda j, i, te, nv: (i, j)),
    )
    return pl.pallas_call(
        _g_down_kernel,
        out_shape=jax.ShapeDtypeStruct((m, n), F32),
        grid_spec=grid_spec,
        compiler_params=_cparams(2),
        name="moe_down",
    )(tile_expert, n_valid, a, w)


def _moe_combine_kernel(x_ref, r0_ref, r1_ref, rt_ref, g_ref, o_ref):
    rt = rt_ref[...]
    y = rt[:, 2:3] * r0_ref[...] + rt[:, 3:4] * r1_ref[...]
    o_ref[...] = x_ref[...] + g_ref[...] * y


def _moe_combine(x, r0, r1, route, gate, n_prompt_tiles):
    m, d = x.shape
    row = pl.BlockSpec((ROW_TILE, d), lambda i: (i, 0))
    return pl.pallas_call(
        _moe_combine_kernel,
        out_shape=jax.ShapeDtypeStruct((m, d), F32),
        grid=(m // ROW_TILE,),
        in_specs=[row, row, row, pl.BlockSpec((ROW_TILE, LANES), lambda i: (i, 0)),
                  pl.BlockSpec((ROW_TILE, d), _mod_index(n_prompt_tiles))],
        out_specs=row,
        compiler_params=_cparams(1),
        name="moe_combine",
    )(x, r0, r1, route, gate)


def _qknorm_kernel(q_ref, k_ref, v_ref, gq_ref, gk_ref, qn_ref, kn_ref, kb_ref, vb_ref):
    r = lax.broadcasted_iota(jnp.int32, (LANES, LANES), 0) // A_HEAD_DIM
    c = lax.broadcasted_iota(jnp.int32, (LANES, LANES), 1) // A_HEAD_DIM
    seg = jnp.where(r == c, 1.0 / A_HEAD_DIM, 0.0).astype(BF16)

    def norm(x, g):
        hi, lo = _split(x * x)
        ms = (jnp.dot(hi, seg, preferred_element_type=F32)
              + jnp.dot(lo, seg, preferred_element_type=F32))
        return x * lax.rsqrt(ms + RMS_EPS) * g

    for h in range(q_ref.shape[1] // LANES):
        sl = slice(h * LANES, (h + 1) * LANES)
        qn = norm(q_ref[:, sl], gq_ref[...])
        qn_ref[:, sl] = (qn * (A_HEAD_DIM ** -0.5)).astype(BF16)
        kn = norm(k_ref[:, sl], gk_ref[...])
        kn_ref[:, sl] = kn
        kb_ref[:, sl] = kn.astype(BF16)
    vb_ref[...] = v_ref[...].astype(BF16)


def _qknorm(proj, gq, gk, w_a):
    m = proj.shape[0]
    blk = lambda c: pl.BlockSpec((ROW_TILE, w_a), lambda i, c=c: (i, c))
    g_spec = pl.BlockSpec((1, LANES), lambda i: (0, 0))
    out = pl.BlockSpec((ROW_TILE, w_a), lambda i: (i, 0))
    return pl.pallas_call(
        _qknorm_kernel,
        out_shape=(jax.ShapeDtypeStruct((m, w_a), BF16), jax.ShapeDtypeStruct((m, w_a), F32),
                   jax.ShapeDtypeStruct((m, w_a), BF16), jax.ShapeDtypeStruct((m, w_a), BF16)),
        grid=(m // ROW_TILE,),
        in_specs=[blk(0), blk(1), blk(2), g_spec, g_spec],
        out_specs=(out, out, out, out),
        compiler_params=_cparams(1),
        name="qk_norm",
    )(proj, proj, proj, gq, gk)


def _stack_maps(q):
    lane = lax.broadcasted_iota(jnp.int32, q.shape, 1)
    zero = jnp.zeros_like(q)
    return jnp.where(lane < A_HEAD_DIM, q, zero), jnp.where(lane >= A_HEAD_DIM, q, zero)


def _attn_finish(acc, l, tq, lam, g):
    o = acc[:tq] / l[:tq] - lam * (acc[tq:] / l[tq:])
    return _rms(o) * g


def _attn_prompt_kernel(q_ref, k_ref, v_ref, g_ref, lam_ref, o_ref, qs_sc, m_sc, l_sc, acc_sc, *, tq):
    i = pl.program_id(1)
    q1, q2 = _stack_maps(q_ref[...])
    qs_sc[0:tq, :] = q1
    qs_sc[tq:2 * tq, :] = q2
    m_sc[...] = jnp.full(m_sc.shape, NEG_INF, F32)
    l_sc[...] = jnp.zeros(l_sc.shape, F32)
    acc_sc[...] = jnp.zeros(acc_sc.shape, F32)

    def step(j, masked):
        start = pl.multiple_of(j * tq, tq)
        kb = k_ref[pl.ds(start, tq), :]
        vb = v_ref[pl.ds(start, tq), :]
        s = lax.dot_general(qs_sc[...], kb, NT_DIMS, preferred_element_type=F32)
        if masked:
            row = lax.broadcasted_iota(jnp.int32, s.shape, 0)
            col = lax.broadcasted_iota(jnp.int32, s.shape, 1)
            s = jnp.where(col // CHUNK <= (row % tq) // CHUNK, s, NEG_INF)
        m_prev = m_sc[...]
        m_next = jnp.maximum(m_prev, jnp.max(s, axis=1, keepdims=True))
        p = jnp.exp(s - m_next[:, 0:1])
        alpha = jnp.exp(m_prev - m_next)
        l_sc[...] = alpha * l_sc[...] + jnp.sum(p, axis=1, keepdims=True)
        acc_sc[...] = alpha * acc_sc[...] + jnp.dot(p.astype(BF16), vb, preferred_element_type=F32)
        m_sc[...] = m_next

    def body(j, carry):
        step(j, False)
        return carry

    lax.fori_loop(0, i, body, 0)
    step(i, True)
    o_ref[...] = _attn_finish(acc_sc[...], l_sc[...], tq, lam_ref[...], g_ref[...]).astype(o_ref.dtype)


def _attn_prompt(qn, kb, vb, g, lam, t, n_heads):
    tq = ATTN_TILE
    kv_spec = pl.BlockSpec((t, A_VDIM), lambda h, i: (0, h))
    vec = pl.BlockSpec((1, LANES), lambda h, i: (0, 0))
    return pl.pallas_call(
        functools.partial(_attn_prompt_kernel, tq=tq),
        out_shape=jax.ShapeDtypeStruct((t, n_heads * A_VDIM), BF16),
        grid=(n_heads, t // tq),
        in_specs=[pl.BlockSpec((tq, A_VDIM), lambda h, i: (i, h)), kv_spec, kv_spec, vec, vec],
        out_specs=pl.BlockSpec((tq, A_VDIM), lambda h, i: (i, h)),
        scratch_shapes=[pltpu.VMEM((2 * tq, A_VDIM), BF16), pltpu.VMEM((2 * tq, LANES), F32),
                        pltpu.VMEM((2 * tq, LANES), F32), pltpu.VMEM((2 * tq, A_VDIM), F32)],
        compiler_params=_cparams(2),
        name="attn_prompt",
    )(qn, kb, vb, g, lam)


def _attn_sample_kernel(q_ref, k_ref, v_ref, g_ref, lam_ref, o_ref, *, tq, n_past, n_keys):
    q1, q2 = _stack_maps(q_ref[...])
    qs = jnp.concatenate([q1, q2], axis=0)
    s = lax.dot_general(qs, k_ref[0], NT_DIMS, preferred_element_type=F32)
    row = lax.broadcasted_iota(jnp.int32, s.shape, 0)
    col = lax.broadcasted_iota(jnp.int32, s.shape, 1)
    visible = (col // CHUNK <= (n_past + row % tq) // CHUNK) & (col < n_keys)
    s = jnp.where(visible, s, NEG_INF)
    p = jnp.exp(s - jnp.max(s, axis=1, keepdims=True))
    p = jnp.where(col < n_keys, p, 0.0)
    l = jnp.sum(p, axis=1, keepdims=True)
    acc = jnp.dot(p.astype(BF16), v_ref[0], preferred_element_type=F32)
    o_ref[...] = _attn_finish(acc, l, tq, lam_ref[...], g_ref[...]).astype(o_ref.dtype)


def _attn_sample(qn, k_all, v_all, g, lam, row0, n_batch, tq, n_heads, n_past, n_keys):
    kp = k_all.shape[1]
    q_blk0 = row0 // tq
    kv_spec = pl.BlockSpec((1, kp, A_VDIM), lambda b, h: (b, 0, h))
    vec = pl.BlockSpec((1, LANES), lambda b, h: (0, 0))
    return pl.pallas_call(
        functools.partial(_attn_sample_kernel, tq=tq, n_past=n_past, n_keys=n_keys),
        out_shape=jax.ShapeDtypeStruct((n_batch * tq, n_heads * A_VDIM), BF16),
        grid=(n_batch, n_heads),
        in_specs=[pl.BlockSpec((tq, A_VDIM), lambda b, h: (q_blk0 + b, h)), kv_spec, kv_spec, vec, vec],
        out_specs=pl.BlockSpec((tq, A_VDIM), lambda b, h: (b, h)),
        compiler_params=_cparams(2),
        name="attn_sample",
    )(qn, k_all, v_all, g, lam)


def _hgrn_kernel(q_ref, f_ref, i_ref, g_ref, lb_ref, og_ref, s0_ref, o_ref, sfin_ref, s_sc, *, n_heads):
    c = pl.program_id(1)
    L = q_ref.shape[0]
    dk = B_HEAD_DIM

    @pl.when(c == 0)
    def _():
        s_sc[...] = s0_ref[0]

    qf = _silu(q_ref[...])
    lb = lb_ref[...]
    f = lb + (1.0 - lb) * _sigmoid(f_ref[...])
    lf = jnp.log(f)
    kf = 1.0 - f
    vv = i_ref[...]
    gate = _silu(g_ref[...])

    row = lax.broadcasted_iota(jnp.int32, (L, L), 0)
    col = lax.broadcasted_iota(jnp.int32, (L, L), 1)
    roww = lax.broadcasted_iota(jnp.int32, qf.shape, 0)
    b = _dot_exact(jnp.where(col <= row, 1.0, 0.0), lf)

    atts = [jnp.zeros((L, L), F32) for _ in range(n_heads)]
    for lev in range(int(math.log2(L))):
        half = 1 << lev
        mid_prev = (row >> (lev + 1) << (lev + 1)) + (half - 1)
        r = _dot_exact(jnp.where(col == mid_prev, 1.0, 0.0), b)
        upper = ((roww >> lev) & 1) == 1
        qk = jnp.where(upper, qf * jnp.exp(jnp.where(upper, b - r, 0.0)), 0.0)
        kk = jnp.where(upper, 0.0, kf * jnp.exp(jnp.where(upper, 0.0, r - b)))
        same_block = (row >> (lev + 1)) == (col >> (lev + 1))
        for h in range(n_heads):
            sl = slice(h * dk, (h + 1) * dk)
            atts[h] = atts[h] + jnp.where(same_block, _dot3(qk[:, sl], kk[:, sl], NT_DIMS), 0.0)

    b_last = b[L - 1:L, :]
    qe = qf * jnp.exp(b)
    kd = kf * jnp.exp(b_last - b)
    e_last = jnp.exp(b_last)
    eye = (lax.broadcasted_iota(jnp.int32, (dk, dk), 0) == lax.broadcasted_iota(jnp.int32, (dk, dk), 1))
    for h in range(n_heads):
        sl = slice(h * dk, (h + 1) * dk)
        qh, kh, vh = qf[:, sl], kf[:, sl], vv[:, sl]
        att = atts[h] + jnp.where(row == col, jnp.sum(qh * kh, axis=1, keepdims=True), 0.0)
        s_prev = s_sc[h]
        o = _dot3(att, vh) + _dot3(qe[:, sl], s_prev)
        decay = jnp.where(eye, jnp.broadcast_to(e_last[:, sl], (dk, dk)), 0.0)
        s_sc[h] = _dot3(decay, s_prev) + _dot3(kd[:, sl], vh, TN_DIMS)
        o_ref[:, sl] = (_rms(o) * og_ref[...] * gate[:, sl]).astype(o_ref.dtype)
    sfin_ref[0] = s_sc[...]


def _hgrn(proj, lb, og, s0, col0, row_blk0, n_batch, n_chunks, chunk, w_b):
    n_heads = w_b // B_HEAD_DIM
    cb = col0 // w_b
    blk = lambda k: pl.BlockSpec((chunk, w_b), lambda b, c, k=k: (row_blk0 + b * n_chunks + c, cb + k))
    vec = lambda n: pl.BlockSpec((1, n), lambda b, c: (0, 0))
    st = pl.BlockSpec((1, n_heads, B_HEAD_DIM, B_HEAD_DIM), lambda b, c: (b, 0, 0, 0))
    return pl.pallas_call(
        functools.partial(_hgrn_kernel, n_heads=n_heads),
        out_shape=(jax.ShapeDtypeStruct((n_batch * n_chunks * chunk, w_b), BF16),
                   jax.ShapeDtypeStruct(s0.shape, F32)),
        grid=(n_batch, n_chunks),
        in_specs=[blk(0), blk(1), blk(2), blk(3), vec(w_b), vec(B_HEAD_DIM), st],
        out_specs=(pl.BlockSpec((chunk, w_b), lambda b, c: (b * n_chunks + c, 0)), st),
        scratch_shapes=[pltpu.VMEM((n_heads, B_HEAD_DIM, B_HEAD_DIM), F32)],
        compiler_params=_cparams(2),
        name="hgrn2",
    )(proj, proj, proj, proj, lb, og, s0)


def _gelu_tanh(x):
    return 0.5 * x * (1.0 + jnp.tanh(math.sqrt(2.0 / math.pi) * (x + 0.044715 * x * x * x)))


def _s5_kernel(u_ref, bre_ref, bim_ref, cre_ref, cim_ref, d_ref, tab_ref, x0re_ref, x0im_ref,
               glu_ref, og_ref, o_ref, fre_ref, fim_ref, xre_sc, xim_sc, cr_sc, ci_sc):
    t = pl.program_id(1)
    rows, w_c = u_ref.shape
    n_state = xre_sc.shape[1]
    n_blk = w_c // LANES
    sb = n_state // n_blk

    @pl.when(t == 0)
    def _():
        cr_sc[...] = jnp.broadcast_to(x0re_ref[0], cr_sc.shape)
        ci_sc[...] = jnp.broadcast_to(x0im_ref[0], ci_sc.shape)

    u = u_ref[...]
    for j in range(n_blk):
        uj = u[:, j * LANES:(j + 1) * LANES]
        xre_sc[:, j * sb:(j + 1) * sb] = _dot3(uj, bre_ref[j])
        xim_sc[:, j * sb:(j + 1) * sb] = _dot3(uj, bim_ref[j])

    half = n_state // 2

    def body(r, carry):
        r0 = pl.multiple_of(r * SUBLANES, SUBLANES)
        for hs in range(2):
            cs = slice(hs * half, (hs + 1) * half)
            xr = xre_sc[pl.ds(r0, SUBLANES), cs]
            xi = xim_sc[pl.ds(r0, SUBLANES), cs]
            for lev, shift in enumerate((1, 2, 4)):
                ar = tab_ref[2 * lev, :, cs]
                ai = tab_ref[2 * lev + 1, :, cs]
                sr = pltpu.roll(xr, shift, 0)
                si = pltpu.roll(xi, shift, 0)
                xr, xi = xr + (ar * sr - ai * si), xi + (ar * si + ai * sr)
            pr = tab_ref[6, :, cs]
            pi = tab_ref[7, :, cs]
            cr = cr_sc[:, cs]
            ci = ci_sc[:, cs]
            xr, xi = xr + (pr * cr - pi * ci), xi + (pr * ci + pi * cr)
            xre_sc[pl.ds(r0, SUBLANES), cs] = xr
            xim_sc[pl.ds(r0, SUBLANES), cs] = xi
            cr_sc[:, cs] = jnp.broadcast_to(xr[SUBLANES - 1:SUBLANES, :], (SUBLANES, half))
            ci_sc[:, cs] = jnp.broadcast_to(xi[SUBLANES - 1:SUBLANES, :], (SUBLANES, half))
        return carry

    lax.fori_loop(0, rows // SUBLANES, body, 0)

    ys = []
    for j in range(n_blk):
        ys.append(_dot3(xre_sc[:, j * sb:(j + 1) * sb], cre_ref[j])
                  - _dot3(xim_sc[:, j * sb:(j + 1) * sb], cim_ref[j]))
    yc = _gelu_tanh(jnp.concatenate(ys, axis=1) + d_ref[...] * u)
    gl = jnp.dot(yc.astype(BF16), glu_ref[...], preferred_element_type=F32)
    v = gl[:, :w_c] * _sigmoid(gl[:, w_c:])
    o_ref[...] = (_rms(v) * og_ref[...]).astype(o_ref.dtype)
    fre_ref[0] = cr_sc[0:1, :]
    fim_ref[0] = ci_sc[0:1, :]


def _s5(proj, wts, x0re, x0im, col0, row_blk0, n_batch, n_tiles, tile, w_c):
    bre, bim, cre, cim, dvec, tab, glu, og = wts
    n_state = tab.shape[2]
    cb = col0 // w_c
    full = lambda a: pl.BlockSpec(a.shape, lambda b, t, nd=a.ndim: (0,) * nd)
    st = pl.BlockSpec((1, 1, n_state), lambda b, t: (b, 0, 0))
    return pl.pallas_call(
        _s5_kernel,
        out_shape=(jax.ShapeDtypeStruct((n_batch * n_tiles * tile, w_c), BF16),
                   jax.ShapeDtypeStruct((n_batch, 1, n_state), F32),
                   jax.ShapeDtypeStruct((n_batch, 1, n_state), F32)),
        grid=(n_batch, n_tiles),
        in_specs=[pl.BlockSpec((tile, w_c), lambda b, t: (row_blk0 + b * n_tiles + t, cb)),
                  full(bre), full(bim), full(cre), full(cim), full(dvec), full(tab), st, st,
                  full(glu), full(og)],
        out_specs=(pl.BlockSpec((tile, w_c), lambda b, t: (b * n_tiles + t, 0)), st, st),
        scratch_shapes=[pltpu.VMEM((tile, n_state), F32), pltpu.VMEM((tile, n_state), F32),
                        pltpu.VMEM((SUBLANES, n_state), F32), pltpu.VMEM((SUBLANES, n_state), F32)],
        compiler_params=_cparams(2),
        name="s5",
    )(proj, bre, bim, cre, cim, dvec, tab, x0re, x0im, glu, og)


def _s5_weights(a_re, a_im, b_re, b_im, c_re, c_im, d, log_dt, glu_w, out_g):
    g, n = a_re.shape
    p = b_re.shape[2]
    lam_re = jnp.minimum(a_re.astype(F32), A_RE_MAX)
    lam_im = a_im.astype(F32)
    dt = jnp.exp(log_dt.astype(F32))[:, None]
    mag = jnp.exp(lam_re * dt)
    abar_re = mag * jnp.cos(lam_im * dt)
    abar_im = mag * jnp.sin(lam_im * dt)
    den = lam_re * lam_re + lam_im * lam_im
    z_re = abar_re - 1.0
    coef_re = (z_re * lam_re + abar_im * lam_im) / den
    coef_im = (abar_im * lam_re - z_re * lam_im) / den
    b_re = b_re.astype(F32)
    b_im = b_im.astype(F32)
    bbar_re = coef_re[..., None] * b_re - coef_im[..., None] * b_im
    bbar_im = coef_re[..., None] * b_im + coef_im[..., None] * b_re

    gpb = LANES // p
    n_blk = g // gpb
    eye = jnp.eye(gpb, dtype=F32)

    def blockdiag_in(bb):
        x = bb.reshape(n_blk, gpb, n, p)
        return jnp.einsum('jgnp,gh->jgphn', x, eye).reshape(n_blk, gpb * p, gpb * n)

    def blockdiag_out(cc):
        x = cc.astype(F32).reshape(n_blk, gpb, p, n)
        return jnp.einsum('jgpn,gh->jgnhp', x, eye).reshape(n_blk, gpb * n, gpb * p)

    pw_re, pw_im = [abar_re.reshape(-1)], [abar_im.reshape(-1)]
    for _ in range(SUBLANES - 1):
        pr, pi = pw_re[-1], pw_im[-1]
        pw_re.append(pr * pw_re[0] - pi * pw_im[0])
        pw_im.append(pr * pw_im[0] + pi * pw_re[0])
    rows = jnp.arange(SUBLANES)[:, None]
    tabs = []
    for k in (1, 2, 4):
        keep = rows >= k
        tabs.append(jnp.where(keep, pw_re[k - 1][None, :], 0.0))
        tabs.append(jnp.where(keep, pw_im[k - 1][None, :], 0.0))
    tabs.append(jnp.stack(pw_re))
    tabs.append(jnp.stack(pw_im))
    tab = jnp.stack(tabs)
    return (blockdiag_in(bbar_re), blockdiag_in(bbar_im), blockdiag_out(c_re), blockdiag_out(c_im),
            d.astype(F32).reshape(1, g * p), tab, glu_w.astype(BF16), out_g.astype(F32).reshape(1, -1))


def _moe_plan(route, n_experts, m_pad):
    n_tok = route.shape[0]
    tm = MOE_ROW_TILE
    e_flat = jnp.concatenate([route[:, 0], route[:, 1]]).astype(jnp.int32)
    tok = jnp.concatenate([jnp.arange(n_tok, dtype=jnp.int32)] * TOP_K)
    order = jnp.argsort(e_flat, stable=True)
    sorted_e = e_flat[order]
    counts = jnp.bincount(e_flat, length=n_experts).astype(jnp.int32)
    padded = (counts + tm - 1) // tm * tm
    start_pad = jnp.cumsum(padded) - padded
    start = jnp.cumsum(counts) - counts
    dest = start_pad[sorted_e] + (jnp.arange(TOP_K * n_tok, dtype=jnp.int32) - start[sorted_e])
    row_token = jnp.zeros((m_pad,), jnp.int32).at[dest].set(tok[order])
    pos = jnp.zeros((TOP_K * n_tok,), jnp.int32).at[order].set(dest)
    n_valid = (jnp.sum(padded) // tm).astype(jnp.int32)
    tile_start = jnp.arange(m_pad // tm, dtype=jnp.int32) * tm
    tile_expert = jnp.searchsorted(jnp.cumsum(padded), tile_start, side='right').astype(jnp.int32)
    tile_expert = jnp.minimum(tile_expert, sorted_e[-1])
    return row_token, pos[:n_tok], pos[n_tok:], tile_expert, n_valid.reshape(1)


def _pad_cols(w, n_pad):
    return jnp.pad(w, [(0, 0)] * (w.ndim - 1) + [(0, n_pad - w.shape[-1])])


def _pad_rows(w, k_pad):
    return jnp.pad(w, [(0, 0)] * (w.ndim - 2) + [(0, k_pad - w.shape[-2]), (0, 0)])


def kernel(x_prompt, x_sample, c_prompt, c_sample, cache_k, cache_v, state_hgrn, state_ssm_re, state_ssm_im, w_ada, b_ada, norm1_g, norm2_g, w_in, w_out, q_norm_g, k_norm_g, lambda_qk, attn_out_g, hgrn_lower_bounds, hgrn_out_g, ssm_a_re, ssm_a_im, ssm_b_re, ssm_b_im, ssm_c_re, ssm_c_im, ssm_d, ssm_log_dt, ssm_glu_w, ssm_out_g, ffn_w1, ffn_w3, ffn_w2, moe_router, moe_w1, moe_w3, moe_w2):
    bp, t, d = x_prompt.shape
    bs, ts, _ = x_sample.shape
    depth = w_in.shape[0]
    n_past = cache_k.shape[2]
    n_heads = cache_k.shape[3]
    w_a = n_heads * A_VDIM
    w_b = hgrn_lower_bounds.shape[1]
    w_c = ssm_out_g.shape[1]
    n_state = ssm_a_re.shape[1] * ssm_a_re.shape[2]
    n_sample = bs * ts
    assert bp == 1 and t % ROW_TILE == 0 and t % ATTN_TILE == 0 and t % S5_TILE == 0 and t % CHUNK == 0
    assert n_sample % ROW_TILE == 0 and ROW_TILE % ts == 0 and ts <= CHUNK and ts % 16 == 0
    assert (ts & (ts - 1)) == 0 and w_a == 2 * w_b and w_b == w_c
    n_pt = t // ROW_TILE
    m_tot = t + n_sample

    x = jnp.concatenate([x_prompt.reshape(t, d), x_sample.reshape(n_sample, d)], axis=0)

    c_rows = jnp.concatenate([c_prompt, c_sample], axis=0)
    c_rows = jnp.pad(c_rows, ((0, _round_up(1 + bs, 16) - (1 + bs)), (0, 0)))
    mod = _modulation(c_rows.astype(F32), w_ada, b_ada)

    def mod_blocks(l, k):
        m = mod[l, :, k * d:(k + 1) * d]
        return jnp.concatenate([jnp.broadcast_to(m[0:1], (ROW_TILE, d)),
                                jnp.repeat(m[1:1 + bs], ts, axis=0)], axis=0)

    lb_p = jax.nn.softmax(hgrn_lower_bounds.astype(F32), axis=0)
    lower_bounds = jnp.cumsum(lb_p, axis=0) - lb_p[0]

    zero_hgrn = jnp.zeros((bp,) + state_hgrn.shape[2:], F32)
    zero_ssm = jnp.zeros((bp, 1, n_state), F32)
    n_keys = n_past + ts
    keys_pad = _round_up(n_keys, LANES)

    kp, vp, sp, rep, imp = [], [], [], [], []
    ks, vs, ss, res, ims = [], [], [], [], []
    for l in range(depth):
        sh1, sc1, g1, sh2, sc2, g2 = [mod_blocks(l, k) for k in range(6)]
        h = _norm_mod(x, norm1_g[l].reshape(1, d), sc1, sh1, n_pt)
        proj = _matmul(h, w_in[l].astype(BF16))

        lam_init = 0.8 - 0.6 * math.exp(-0.3 * l)
        lq = lambda_qk[l].astype(F32)
        lam = jnp.exp(jnp.sum(lq[0] * lq[1])) - jnp.exp(jnp.sum(lq[2] * lq[3])) + lam_init
        lam_vec = jnp.full((1, LANES), lam, F32)
        og_a = (attn_out_g[l].astype(F32) * (1.0 - lam_init)).reshape(1, A_VDIM)
        gq = jnp.tile(q_norm_g[l].astype(F32), 2).reshape(1, LANES)
        gk = jnp.tile(k_norm_g[l].astype(F32), 2).reshape(1, LANES)
        qn, kn, kb, vb = _qknorm(proj, gq, gk, w_a)
        oa_p = _attn_prompt(qn, kb, vb, og_a, lam_vec, t, n_heads)
        pad = jnp.zeros((bs, keys_pad - n_keys, w_a), BF16)
        k_all = jnp.concatenate([cache_k[l].reshape(bs, n_past, w_a).astype(BF16),
                                 kb[t:].reshape(bs, ts, w_a), pad], axis=1)
        v_all = jnp.concatenate([cache_v[l].reshape(bs, n_past, w_a).astype(BF16),
                                 vb[t:].reshape(bs, ts, w_a), pad], axis=1)
        oa_s = _attn_sample(qn, k_all, v_all, og_a, lam_vec, t, bs, ts, n_heads, n_past, n_keys)

        lb = lower_bounds[l].reshape(1, w_b)
        og_b = hgrn_out_g[l].astype(F32).reshape(1, B_HEAD_DIM)
        col_b = 3 * w_a
        ob_p, s_p = _hgrn(proj, lb, og_b, zero_hgrn, col_b, 0, bp, t // CHUNK, CHUNK, w_b)
        ob_s, s_s = _hgrn(proj, lb, og_b, state_hgrn[l].astype(F32), col_b, t // ts, bs, 1, ts, w_b)

        wts = _s5_weights(ssm_a_re[l], ssm_a_im[l], ssm_b_re[l], ssm_b_im[l], ssm_c_re[l], ssm_c_im[l],
                          ssm_d[l], ssm_log_dt[l], ssm_glu_w[l], ssm_out_g[l])
        col_c = 3 * w_a + 4 * w_b
        oc_p, re_p, im_p = _s5(proj, wts, zero_ssm, zero_ssm, col_c, 0, bp, t // S5_TILE, S5_TILE, w_c)
        oc_s, re_s, im_s = _s5(proj, wts, state_ssm_re[l].astype(F32).reshape(bs, 1, n_state),
                               state_ssm_im[l].astype(F32).reshape(bs, 1, n_state),
                               col_c, t // ts, bs, 1, ts, w_c)

        merged = jnp.concatenate([jnp.concatenate([oa_p, ob_p, oc_p], axis=1),
                                  jnp.concatenate([oa_s, ob_s, oc_s], axis=1)], axis=0)
        x = _matmul_residual(merged, w_out[l].astype(BF16), x, g1, n_pt)

        if l % 2 == 0:
            j = l // 2
            dff = ffn_w1.shape[2]
            dff_pad = _round_up(dff, 512)
            h2 = _norm_mod(x, norm2_g[l].reshape(1, d), sc2, sh2, n_pt)
            a = _swiglu_up(h2, _pad_cols(ffn_w1[j].astype(BF16), dff_pad), _pad_cols(ffn_w3[j].astype(BF16), dff_pad))
            x = _matmul_residual(a, _pad_rows(ffn_w2[j].astype(BF16), dff_pad), x, g2, n_pt)
        else:
            j = l // 2
            n_experts = moe_router.shape[2]
            dff = moe_w1.shape[3]
            dff_pad = _round_up(dff, 512)
            h2, route = _norm_mod(x, norm2_g[l].reshape(1, d), sc2, sh2, n_pt, w_router=moe_router[j])
            m_pad = TOP_K * m_tot + n_experts * MOE_ROW_TILE
            row_token, pos0, pos1, tile_expert, n_valid = _moe_plan(route, n_experts, m_pad)
            a_sorted = jnp.take(h2, row_token, axis=0)
            up = _grouped_swiglu_up(a_sorted, _pad_cols(moe_w1[j].astype(BF16), dff_pad),
                                    _pad_cols(moe_w3[j].astype(BF16), dff_pad), tile_expert, n_valid)
            down = _grouped_down(up, _pad_rows(moe_w2[j].astype(BF16), dff_pad), tile_expert, n_valid)
            x = _moe_combine(x, jnp.take(down, pos0, axis=0), jnp.take(down, pos1, axis=0), route, g2, n_pt)

        v_all_tokens = proj[:, 2 * w_a:3 * w_a]
        kp.append(kn[:t].reshape(bp, t, n_heads, 2, A_HEAD_DIM))
        vp.append(v_all_tokens[:t].reshape(bp, t, n_heads, A_VDIM))
        sp.append(s_p)
        rep.append(re_p.reshape(bp, -1, C_STATE))
        imp.append(im_p.reshape(bp, -1, C_STATE))
        ks.append(kn[t:].reshape(bs, ts, n_heads, 2, A_HEAD_DIM))
        vs.append(v_all_tokens[t:].reshape(bs, ts, n_heads, A_VDIM))
        ss.append(s_s)
        res.append(re_s.reshape(bs, -1, C_STATE))
        ims.append(im_s.reshape(bs, -1, C_STATE))

    return (x[:t].reshape(bp, t, d), x[t:].reshape(bs, ts, d),
            jnp.stack(kp), jnp.stack(vp), jnp.stack(sp), jnp.stack(rep), jnp.stack(imp),
            jnp.stack(ks), jnp.stack(vs), jnp.stack(ss), jnp.stack(res), jnp.stack(ims))
```

```python
import functools
import math

import numpy as np
import jax
import jax.numpy as jnp
from jax import lax
from jax.experimental import pallas as pl
from jax.experimental.pallas import tpu as pltpu

F32 = jnp.float32
BF16 = jnp.bfloat16
HIGHEST = lax.Precision.HIGHEST

CHUNK = 64
A_HEAD_DIM = 64
A_VDIM = 2 * A_HEAD_DIM
B_HEAD_DIM = 128
C_GROUP = 16
C_STATE = 64
TOP_K = 2
RMS_EPS = 1e-6
NEG_INF = -1e30
A_RE_MAX = -1e-4

LANES = 128
SUBLANES = 8
ROW_TILE = 512
MOE_ROW_TILE = 256
ATTN_TILE = 512
ATTN_FAST_TILE = 1024
ATTN_SHIFT_MAX = 60.0
HGRN_TILE = 256
S5_TILE = 512
VMEM_LIMIT = 48 * 1024 * 1024

Q_SCALE = (A_HEAD_DIM ** -0.5) * math.log2(math.e)

NT_DIMS = (((1,), (1,)), ((), ()))
TN_DIMS = (((0,), (0,)), ((), ()))
NN_DIMS = (((1,), (0,)), ((), ()))


def _cparams(n_axes):
    return pltpu.CompilerParams(
        dimension_semantics=("arbitrary",) * n_axes, vmem_limit_bytes=VMEM_LIMIT)


def _round_up(x, m):
    return (x + m - 1) // m * m


def _pick_tile(n, target):
    best = LANES
    for t in range(LANES, min(n, target) + 1, LANES):
        if n % t == 0:
            best = t
    return best


def _split(x):
    hi = x.astype(BF16)
    lo = (x - hi.astype(F32)).astype(BF16)
    return hi, lo


def _dot3(a, b, dims=NN_DIMS):
    ah, al = _split(a)
    bh, bl = _split(b)
    d = functools.partial(lax.dot_general, dimension_numbers=dims, preferred_element_type=F32)
    return d(ah, bh) + d(al, bh) + d(ah, bl)


def _dot_exact(a, b):
    return jnp.dot(a, b, precision=HIGHEST, preferred_element_type=F32)


def _sigmoid(x):
    return 1.0 / (1.0 + jnp.exp(-x))


def _silu(x):
    return x * _sigmoid(x)


def _rms(x):
    return x * lax.rsqrt(jnp.mean(x * x, axis=-1, keepdims=True) + RMS_EPS)


def _mod_kernel(c_ref, w_ref, b_ref, o_ref):
    o_ref[0] = _dot3(_silu(c_ref[...]), w_ref[0]) + b_ref[0]


def _modulation(c_rows, w_ada, b_ada):
    depth, d, n = w_ada.shape
    rows = c_rows.shape[0]
    tn = _pick_tile(n, 1024)
    return pl.pallas_call(
        _mod_kernel,
        out_shape=jax.ShapeDtypeStruct((depth, rows, n), F32),
        grid=(depth, n // tn),
        in_specs=[
            pl.BlockSpec((rows, d), lambda l, j: (0, 0)),
            pl.BlockSpec((1, d, tn), lambda l, j: (l, 0, j)),
            pl.BlockSpec((1, 1, tn), lambda l, j: (l, 0, j)),
        ],
        out_specs=pl.BlockSpec((1, rows, tn), lambda l, j: (l, 0, j)),
        compiler_params=_cparams(2),
        name="adaln_mod",
    )(c_rows, w_ada, b_ada.reshape(depth, 1, n))


def _norm_mod_kernel(x_ref, g_ref, sc_ref, sh_ref, h_ref):
    y = _rms(x_ref[...]) * g_ref[...]
    h_ref[...] = (y * (1.0 + sc_ref[...]) + sh_ref[...]).astype(h_ref.dtype)


def _norm_mod_route_kernel(x_ref, g_ref, sc_ref, sh_ref, wr_ref, h_ref, r_ref, *, n_experts):
    y = _rms(x_ref[...]) * g_ref[...]
    h = y * (1.0 + sc_ref[...]) + sh_ref[...]
    h_ref[...] = h.astype(h_ref.dtype)
    logits = _dot3(h, wr_ref[...])
    lane = lax.broadcasted_iota(jnp.int32, logits.shape, 1).astype(F32)
    lg = jnp.where(lane < n_experts, logits, -jnp.inf)
    m1 = jnp.max(lg, axis=1, keepdims=True)
    i1 = jnp.min(jnp.where(lg == m1, lane, float(LANES)), axis=1, keepdims=True)
    lg2 = jnp.where(lane == i1, -jnp.inf, lg)
    m2 = jnp.max(lg2, axis=1, keepdims=True)
    i2 = jnp.min(jnp.where(lg2 == m2, lane, float(LANES)), axis=1, keepdims=True)
    e = jnp.exp(m2 - m1)
    inv = 1.0 / (1.0 + e)
    r_ref[...] = jnp.where(lane == 0.0, i1, jnp.where(lane == 1.0, i2, jnp.where(
        lane == 2.0, inv, jnp.where(lane == 3.0, e * inv, 0.0))))


def _mod_index(n_prompt_tiles):
    return lambda i: (jnp.maximum(i - n_prompt_tiles + 1, 0), 0)


def _norm_mod(x, g, sc, sh, n_prompt_tiles, w_router=None):
    m, d = x.shape
    mi = _mod_index(n_prompt_tiles)
    in_specs = [
        pl.BlockSpec((ROW_TILE, d), lambda i: (i, 0)),
        pl.BlockSpec((1, d), lambda i: (0, 0)),
        pl.BlockSpec((ROW_TILE, d), mi),
        pl.BlockSpec((ROW_TILE, d), mi),
    ]
    h_spec = pl.BlockSpec((ROW_TILE, d), lambda i: (i, 0))
    if w_router is None:
        return pl.pallas_call(
            _norm_mod_kernel,
            out_shape=jax.ShapeDtypeStruct((m, d), BF16),
            grid=(m // ROW_TILE,),
            in_specs=in_specs,
            out_specs=h_spec,
            compiler_params=_cparams(1),
            name="norm_mod",
        )(x, g, sc, sh)
    n_experts = w_router.shape[1]
    wr = jnp.zeros((d, LANES), F32).at[:, :n_experts].set(w_router.astype(F32))
    return pl.pallas_call(
        functools.partial(_norm_mod_route_kernel, n_experts=n_experts),
        out_shape=(jax.ShapeDtypeStruct((m, d), BF16), jax.ShapeDtypeStruct((m, LANES), F32)),
        grid=(m // ROW_TILE,),
        in_specs=in_specs + [pl.BlockSpec((d, LANES), lambda i: (0, 0))],
        out_specs=(h_spec, pl.BlockSpec((ROW_TILE, LANES), lambda i: (i, 0))),
        compiler_params=_cparams(1),
        name="norm_mod_route",
    )(x, g, sc, sh, wr)


def _mm_kernel(a_ref, w_ref, o_ref):
    o_ref[...] = jnp.dot(a_ref[...], w_ref[...], preferred_element_type=F32).astype(o_ref.dtype)


def _mm_res_kernel(a_ref, w_ref, x_ref, g_ref, o_ref):
    acc = jnp.dot(a_ref[...], w_ref[...], preferred_element_type=F32)
    o_ref[...] = x_ref[...] + g_ref[...] * acc


def _swiglu_up_kernel(a_ref, w1_ref, w3_ref, o_ref):
    a = a_ref[...]
    u = jnp.dot(a, w1_ref[...], preferred_element_type=F32)
    v = jnp.dot(a, w3_ref[...], preferred_element_type=F32)
    o_ref[...] = (_silu(u) * v).astype(o_ref.dtype)


def _matmul(a, w, tn_target=1408):
    m, k = a.shape
    n = w.shape[1]
    tn = _pick_tile(n, tn_target)
    return pl.pallas_call(
        _mm_kernel,
        out_shape=jax.ShapeDtypeStruct((m, n), F32),
        grid=(n // tn, m // ROW_TILE),
        in_specs=[
            pl.BlockSpec((ROW_TILE, k), lambda j, i: (i, 0)),
            pl.BlockSpec((k, tn), lambda j, i: (0, j)),
        ],
        out_specs=pl.BlockSpec((ROW_TILE, tn), lambda j, i: (i, j)),
        compiler_params=_cparams(2),
        name="matmul",
    )(a, w)


def _matmul_residual(a, w, x, gate, n_prompt_tiles, tn_target=512):
    m = a.shape[0]
    k, n = w.shape
    tn = _pick_tile(n, tn_target)
    return pl.pallas_call(
        _mm_res_kernel,
        out_shape=jax.ShapeDtypeStruct((m, n), F32),
        grid=(n // tn, m // ROW_TILE),
        in_specs=[
            pl.BlockSpec((ROW_TILE, k), lambda j, i: (i, 0)),
            pl.BlockSpec((k, tn), lambda j, i: (0, j)),
            pl.BlockSpec((ROW_TILE, tn), lambda j, i: (i, j)),
            pl.BlockSpec((ROW_TILE, tn), lambda j, i: (jnp.maximum(i - n_prompt_tiles + 1, 0), j)),
        ],
        out_specs=pl.BlockSpec((ROW_TILE, tn), lambda j, i: (i, j)),
        compiler_params=_cparams(2),
        name="matmul_residual",
    )(a, w, x, gate)


def _swiglu_up(a, w1, w3, tn_target=1408):
    m, k = a.shape
    n = w1.shape[1]
    tn = _pick_tile(n, tn_target)
    return pl.pallas_call(
        _swiglu_up_kernel,
        out_shape=jax.ShapeDtypeStruct((m, n), BF16),
        grid=(n // tn, m // ROW_TILE),
        in_specs=[
            pl.BlockSpec((ROW_TILE, k), lambda j, i: (i, 0)),
            pl.BlockSpec((k, tn), lambda j, i: (0, j)),
            pl.BlockSpec((k, tn), lambda j, i: (0, j)),
        ],
        out_specs=pl.BlockSpec((ROW_TILE, tn), lambda j, i: (i, j)),
        compiler_params=_cparams(2),
        name="swiglu_up",
    )(a, w1, w3)


def _g_up_kernel(te_ref, nv_ref, a_ref, w1_ref, w3_ref, o_ref):
    i = pl.program_id(1)

    @pl.when(i < nv_ref[0])
    def _():
        a = a_ref[...]
        u = jnp.dot(a, w1_ref[0], preferred_element_type=F32)
        v = jnp.dot(a, w3_ref[0], preferred_element_type=F32)
        o_ref[...] = (_silu(u) * v).astype(o_ref.dtype)

    @pl.when(i >= nv_ref[0])
    def _():
        o_ref[...] = jnp.zeros_like(o_ref)


def _g_down_kernel(te_ref, nv_ref, a_ref, w_ref, o_ref):
    i = pl.program_id(1)

    @pl.when(i < nv_ref[0])
    def _():
        o_ref[...] = jnp.dot(a_ref[...], w_ref[0], preferred_element_type=F32)

    @pl.when(i >= nv_ref[0])
    def _():
        o_ref[...] = jnp.zeros_like(o_ref)


def _grouped_swiglu_up(a, w1, w3, tile_expert, n_valid, tn_target=1408):
    m, k = a.shape
    n = w1.shape[2]
    tn = _pick_tile(n, tn_target)
    tm = MOE_ROW_TILE
    grid_spec = pltpu.PrefetchScalarGridSpec(
        num_scalar_prefetch=2,
        grid=(n // tn, m // tm),
        in_specs=[
            pl.BlockSpec((tm, k), lambda j, i, te, nv: (i, 0)),
            pl.BlockSpec((1, k, tn), lambda j, i, te, nv: (te[i], 0, j)),
            pl.BlockSpec((1, k, tn), lambda j, i, te, nv: (te[i], 0, j)),
        ],
        out_specs=pl.BlockSpec((tm, tn), lambda j, i, te, nv: (i, j)),
    )
    return pl.pallas_call(
        _g_up_kernel,
        out_shape=jax.ShapeDtypeStruct((m, n), BF16),
        grid_spec=grid_spec,
        compiler_params=_cparams(2),
        name="moe_swiglu_up",
    )(tile_expert, n_valid, a, w1, w3)


def _grouped_down(a, w, tile_expert, n_valid, tn_target=512):
    m = a.shape[0]
    _, k, n = w.shape
    tn = _pick_tile(n, tn_target)
    tm = MOE_ROW_TILE
    grid_spec = pltpu.PrefetchScalarGridSpec(
        num_scalar_prefetch=2,
        grid=(n // tn, m // tm),
        in_specs=[
            pl.BlockSpec((tm, k), lambda j, i, te, nv: (i, 0)),
            pl.BlockSpec((1, k, tn), lambda j, i, te, nv: (te[i], 0, j)),
        ],
        out_specs=pl.BlockSpec((tm, tn), lambda j, i, te, nv: (i, j)),
    )
    return pl.pallas_call(
        _g_down_kernel,
        out_shape=jax.ShapeDtypeStruct((m, n), F32),
        grid_spec=grid_spec,
        compiler_params=_cparams(2),
        name="moe_down",
    )(tile_expert, n_valid, a, w)


def _moe_combine_kernel(x_ref, r0_ref, r1_ref, rt_ref, g_ref, o_ref):
    rt = rt_ref[...]
    y = rt[:, 2:3] * r0_ref[...] + rt[:, 3:4] * r1_ref[...]
    o_ref[...] = x_ref[...] + g_ref[...] * y


def _moe_combine(x, r0, r1, route, gate, n_prompt_tiles):
    m, d = x.shape
    row = pl.BlockSpec((ROW_TILE, d), lambda i: (i, 0))
    return pl.pallas_call(
        _moe_combine_kernel,
        out_shape=jax.ShapeDtypeStruct((m, d), F32),
        grid=(m // ROW_TILE,),
        in_specs=[row, row, row, pl.BlockSpec((ROW_TILE, LANES), lambda i: (i, 0)),
                  pl.BlockSpec((ROW_TILE, d), _mod_index(n_prompt_tiles))],
        out_specs=row,
        compiler_params=_cparams(1),
        name="moe_combine",
    )(x, r0, r1, route, gate)


def _qknorm_kernel(q_ref, k_ref, v_ref, gq_ref, gk_ref, qn_ref, kn_ref, kb_ref, vb_ref):
    r = lax.broadcasted_iota(jnp.int32, (LANES, LANES), 0) // A_HEAD_DIM
    c = lax.broadcasted_iota(jnp.int32, (LANES, LANES), 1) // A_HEAD_DIM
    seg = jnp.where(r == c, 1.0 / A_HEAD_DIM, 0.0).astype(BF16)

    def norm(x, g):
        hi, lo = _split(x * x)
        ms = (jnp.dot(hi, seg, preferred_element_type=F32)
              + jnp.dot(lo, seg, preferred_element_type=F32))
        return x * lax.rsqrt(ms + RMS_EPS) * g

    for h in range(q_ref.shape[1] // LANES):
        sl = slice(h * LANES, (h + 1) * LANES)
        qn = norm(q_ref[:, sl], gq_ref[...])
        qn_ref[:, sl] = (qn * Q_SCALE).astype(BF16)
        kn = norm(k_ref[:, sl], gk_ref[...])
        kn_ref[:, sl] = kn
        kb_ref[:, sl] = kn.astype(BF16)
    vb_ref[...] = v_ref[...].astype(BF16)


def _qknorm(proj, gq, gk, w_a):
    m = proj.shape[0]
    blk = lambda c: pl.BlockSpec((ROW_TILE, w_a), lambda i, c=c: (i, c))
    g_spec = pl.BlockSpec((1, LANES), lambda i: (0, 0))
    out = pl.BlockSpec((ROW_TILE, w_a), lambda i: (i, 0))
    return pl.pallas_call(
        _qknorm_kernel,
        out_shape=(jax.ShapeDtypeStruct((m, w_a), BF16), jax.ShapeDtypeStruct((m, w_a), F32),
                   jax.ShapeDtypeStruct((m, w_a), BF16), jax.ShapeDtypeStruct((m, w_a), BF16)),
        grid=(m // ROW_TILE,),
        in_specs=[blk(0), blk(1), blk(2), g_spec, g_spec],
        out_specs=(out, out, out, out),
        compiler_params=_cparams(1),
        name="qk_norm",
    )(proj, proj, proj, gq, gk)


def _stack_maps(q):
    lane = lax.broadcasted_iota(jnp.int32, q.shape, 1)
    zero = jnp.zeros_like(q)
    return jnp.where(lane < A_HEAD_DIM, q, zero), jnp.where(lane >= A_HEAD_DIM, q, zero)


def _attn_finish(acc, l, tq, lam, g):
    o = acc[:tq] / l[:tq] - lam * (acc[tq:] / l[tq:])
    return _rms(o) * g


def _attn_prompt_kernel(q_ref, k_ref, v_ref, g_ref, lam_ref, o_ref, qs_sc, m_sc, l_sc, acc_sc, *, tq):
    i = pl.program_id(1)
    q1, q2 = _stack_maps(q_ref[...])
    qs_sc[0:tq, :] = q1
    qs_sc[tq:2 * tq, :] = q2
    m_sc[...] = jnp.full(m_sc.shape, NEG_INF, F32)
    l_sc[...] = jnp.zeros(l_sc.shape, F32)
    acc_sc[...] = jnp.zeros(acc_sc.shape, F32)

    def step(j, masked):
        start = pl.multiple_of(j * tq, tq)
        kb = k_ref[pl.ds(start, tq), :]
        vb = v_ref[pl.ds(start, tq), :]
        s = lax.dot_general(qs_sc[...], kb, NT_DIMS, preferred_element_type=F32)
        if masked:
            row = lax.broadcasted_iota(jnp.int32, s.shape, 0)
            col = lax.broadcasted_iota(jnp.int32, s.shape, 1)
            s = jnp.where(col // CHUNK <= (row % tq) // CHUNK, s, NEG_INF)
        m_prev = m_sc[...]
        m_next = jnp.maximum(m_prev, jnp.max(s, axis=1, keepdims=True))
        p = jnp.exp2(s - m_next[:, 0:1])
        alpha = jnp.exp2(m_prev - m_next)
        l_sc[...] = alpha * l_sc[...] + jnp.sum(p, axis=1, keepdims=True)
        acc_sc[...] = alpha * acc_sc[...] + jnp.dot(p.astype(BF16), vb, preferred_element_type=F32)
        m_sc[...] = m_next

    def body(j, carry):
        step(j, False)
        return carry

    lax.fori_loop(0, i, body, 0)
    step(i, True)
    o_ref[...] = _attn_finish(acc_sc[...], l_sc[...], tq, lam_ref[...], g_ref[...]).astype(o_ref.dtype)


def _attn_prompt_shift_kernel(q_ref, k_ref, v_ref, sh_ref, g_ref, lam_ref, o_ref, qs_sc, acc_sc, *, tq):
    i = pl.program_id(1)
    q1, q2 = _stack_maps(q_ref[...])
    qs_sc[0:tq, :] = q1
    qs_sc[tq:2 * tq, :] = q2
    acc_sc[...] = jnp.zeros(acc_sc.shape, F32)
    shift = sh_ref[:, 0:1]
    ones = jnp.ones((tq, LANES), BF16)

    def step(j, masked):
        start = pl.multiple_of(j * tq, tq)
        kb = k_ref[pl.ds(start, tq), :]
        vb1 = jnp.concatenate([v_ref[pl.ds(start, tq), :], ones], axis=1)
        s = lax.dot_general(qs_sc[...], kb, NT_DIMS, preferred_element_type=F32)
        if masked:
            row = lax.broadcasted_iota(jnp.int32, s.shape, 0)
            col = lax.broadcasted_iota(jnp.int32, s.shape, 1)
            s = jnp.where(col // CHUNK <= (row % tq) // CHUNK, s, NEG_INF)
        p = jnp.exp2(s - shift).astype(BF16)
        acc_sc[...] += jnp.dot(p, vb1, preferred_element_type=F32)

    def body(j, carry):
        step(j, False)
        return carry

    lax.fori_loop(0, i, body, 0)
    step(i, True)
    acc = acc_sc[...]
    o_ref[...] = _attn_finish(acc[:, :A_VDIM], acc[:, A_VDIM:], tq, lam_ref[...], g_ref[...]).astype(o_ref.dtype)


def _attn_prompt_running_max(qn, kb, vb, g, lam, t, n_heads):
    tq = ATTN_TILE
    kv_spec = pl.BlockSpec((t, A_VDIM), lambda h, i: (0, h))
    vec = pl.BlockSpec((1, LANES), lambda h, i: (0, 0))
    return pl.pallas_call(
        functools.partial(_attn_prompt_kernel, tq=tq),
        out_shape=jax.ShapeDtypeStruct((t, n_heads * A_VDIM), BF16),
        grid=(n_heads, t // tq),
        in_specs=[pl.BlockSpec((tq, A_VDIM), lambda h, i: (i, h)), kv_spec, kv_spec, vec, vec],
        out_specs=pl.BlockSpec((tq, A_VDIM), lambda h, i: (i, h)),
        scratch_shapes=[pltpu.VMEM((2 * tq, A_VDIM), BF16), pltpu.VMEM((2 * tq, LANES), F32),
                        pltpu.VMEM((2 * tq, LANES), F32), pltpu.VMEM((2 * tq, A_VDIM), F32)],
        compiler_params=_cparams(2),
        name="attn_prompt",
    )(qn, kb, vb, g, lam)


def _attn_prompt_shift(qn, kb, vb, shift, g, lam, t, n_heads):
    tq = ATTN_FAST_TILE
    kv_spec = pl.BlockSpec((t, A_VDIM), lambda h, i: (0, h))
    vec = pl.BlockSpec((1, LANES), lambda h, i: (0, 0))
    return pl.pallas_call(
        functools.partial(_attn_prompt_shift_kernel, tq=tq),
        out_shape=jax.ShapeDtypeStruct((t, n_heads * A_VDIM), BF16),
        grid=(n_heads, t // tq),
        in_specs=[pl.BlockSpec((tq, A_VDIM), lambda h, i: (i, h)), kv_spec, kv_spec, vec, vec, vec],
        out_specs=pl.BlockSpec((tq, A_VDIM), lambda h, i: (i, h)),
        scratch_shapes=[pltpu.VMEM((2 * tq, A_VDIM), BF16), pltpu.VMEM((2 * tq, 2 * A_VDIM), F32)],
        compiler_params=_cparams(2),
        name="attn_prompt_shift",
    )(qn, kb, vb, shift, g, lam)


def _attn_prompt(qn, kb, vb, score_bound, g, lam, t, n_heads):
    shift = jnp.full((1, LANES), score_bound, F32)
    return lax.cond(
        score_bound <= ATTN_SHIFT_MAX,
        lambda: _attn_prompt_shift(qn, kb, vb, shift, g, lam, t, n_heads),
        lambda: _attn_prompt_running_max(qn, kb, vb, g, lam, t, n_heads))


def _attn_sample_kernel(q_ref, k_ref, v_ref, g_ref, lam_ref, o_ref, *, tq, n_past, n_keys):
    q1, q2 = _stack_maps(q_ref[...])
    qs = jnp.concatenate([q1, q2], axis=0)
    s = lax.dot_general(qs, k_ref[0], NT_DIMS, preferred_element_type=F32)
    row = lax.broadcasted_iota(jnp.int32, s.shape, 0)
    col = lax.broadcasted_iota(jnp.int32, s.shape, 1)
    visible = (col // CHUNK <= (n_past + row % tq) // CHUNK) & (col < n_keys)
    s = jnp.where(visible, s, NEG_INF)
    p = jnp.exp2(s - jnp.max(s, axis=1, keepdims=True))
    p = jnp.where(col < n_keys, p, 0.0)
    l = jnp.sum(p, axis=1, keepdims=True)
    acc = jnp.dot(p.astype(BF16), v_ref[0], preferred_element_type=F32)
    o_ref[...] = _attn_finish(acc, l, tq, lam_ref[...], g_ref[...]).astype(o_ref.dtype)


def _attn_sample(qn, k_all, v_all, g, lam, row0, n_batch, tq, n_heads, n_past, n_keys):
    kp = k_all.shape[1]
    q_blk0 = row0 // tq
    kv_spec = pl.BlockSpec((1, kp, A_VDIM), lambda b, h: (b, 0, h))
    vec = pl.BlockSpec((1, LANES), lambda b, h: (0, 0))
    return pl.pallas_call(
        functools.partial(_attn_sample_kernel, tq=tq, n_past=n_past, n_keys=n_keys),
        out_shape=jax.ShapeDtypeStruct((n_batch * tq, n_heads * A_VDIM), BF16),
        grid=(n_batch, n_heads),
        in_specs=[pl.BlockSpec((tq, A_VDIM), lambda b, h: (q_blk0 + b, h)), kv_spec, kv_spec, vec, vec],
        out_specs=pl.BlockSpec((tq, A_VDIM), lambda b, h: (b, h)),
        compiler_params=_cparams(2),
        name="attn_sample",
    )(qn, k_all, v_all, g, lam)


def _hgrn_sum_matrix(L):
    t = np.arange(L)[:, None]
    s = np.arange(L)[None, :]
    mats = [s <= t]
    for lev in range(int(math.log2(L))):
        m = (t >> (lev + 1) << (lev + 1)) + (1 << lev) - 1
        mats.append(np.where(t > m, (s > m) & (s <= t), (s > t) & (s <= m)))
    return jnp.asarray(np.concatenate(mats, axis=0), BF16)


def _hgrn_kernel(q_ref, f_ref, i_ref, g_ref, m_ref, lb_ref, og_ref, s0_ref, o_ref, sfin_ref, s_sc, *, n_heads):
    c = pl.program_id(1)
    L = q_ref.shape[0]
    dk = B_HEAD_DIM

    @pl.when(c == 0)
    def _():
        s_sc[...] = s0_ref[0]

    qf = _silu(q_ref[...])
    lb = lb_ref[...]
    f = lb + (1.0 - lb) * _sigmoid(f_ref[...])
    kf = 1.0 - f
    vv = i_ref[...]
    gate = _silu(g_ref[...])

    lf_hi, lf_lo = _split(jnp.log(f))
    sums = (jnp.dot(m_ref[...], lf_hi, preferred_element_type=F32)
            + jnp.dot(m_ref[...], lf_lo, preferred_element_type=F32))
    b = sums[0:L]

    row = lax.broadcasted_iota(jnp.int32, (L, L), 0)
    col = lax.broadcasted_iota(jnp.int32, (L, L), 1)
    roww = lax.broadcasted_iota(jnp.int32, qf.shape, 0)
    atts = [jnp.zeros((L, L), F32) for _ in range(n_heads)]
    for lev in range(int(math.log2(L))):
        e = jnp.exp(sums[(lev + 1) * L:(lev + 2) * L])
        upper = ((roww >> lev) & 1) == 1
        qk = jnp.where(upper, qf * e, 0.0).astype(BF16)
        kk = jnp.where(upper, 0.0, kf * e).astype(BF16)
        same_block = (row >> (lev + 1)) == (col >> (lev + 1))
        for h in range(n_heads):
            sl = slice(h * dk, (h + 1) * dk)
            a = lax.dot_general(qk[:, sl], kk[:, sl], NT_DIMS, preferred_element_type=F32)
            atts[h] = atts[h] + jnp.where(same_block, a, 0.0)

    b_last = b[L - 1:L, :]
    qe = (qf * jnp.exp(b)).astype(BF16)
    kd = (kf * jnp.exp(b_last - b)).astype(BF16)
    vb = vv.astype(BF16)
    e_last = jnp.exp(b_last)
    eye = (lax.broadcasted_iota(jnp.int32, (dk, dk), 0) == lax.broadcasted_iota(jnp.int32, (dk, dk), 1))
    for h in range(n_heads):
        sl = slice(h * dk, (h + 1) * dk)
        att = atts[h] + jnp.where(row == col, jnp.sum(qf[:, sl] * kf[:, sl], axis=1, keepdims=True), 0.0)
        s_prev = s_sc[h]
        o = (jnp.dot(att.astype(BF16), vb[:, sl], preferred_element_type=F32)
             + jnp.dot(qe[:, sl], s_prev.astype(BF16), preferred_element_type=F32))
        decay = jnp.where(eye, jnp.broadcast_to(e_last[:, sl], (dk, dk)), 0.0)
        s_sc[h] = _dot3(decay, s_prev) + lax.dot_general(kd[:, sl], vb[:, sl], TN_DIMS, preferred_element_type=F32)
        o_ref[:, sl] = (_rms(o) * og_ref[...] * gate[:, sl]).astype(o_ref.dtype)
    sfin_ref[0] = s_sc[...]


def _hgrn(proj, lb, og, s0, col0, row_blk0, n_batch, n_chunks, chunk, w_b):
    n_heads = w_b // B_HEAD_DIM
    cb = col0 // w_b
    sum_mat = _hgrn_sum_matrix(chunk)
    blk = lambda k: pl.BlockSpec((chunk, w_b), lambda b, c, k=k: (row_blk0 + b * n_chunks + c, cb + k))
    vec = lambda n: pl.BlockSpec((1, n), lambda b, c: (0, 0))
    st = pl.BlockSpec((1, n_heads, B_HEAD_DIM, B_HEAD_DIM), lambda b, c: (b, 0, 0, 0))
    return pl.pallas_call(
        functools.partial(_hgrn_kernel, n_heads=n_heads),
        out_shape=(jax.ShapeDtypeStruct((n_batch * n_chunks * chunk, w_b), BF16),
                   jax.ShapeDtypeStruct(s0.shape, F32)),
        grid=(n_batch, n_chunks),
        in_specs=[blk(0), blk(1), blk(2), blk(3), pl.BlockSpec(sum_mat.shape, lambda b, c: (0, 0)),
                  vec(w_b), vec(B_HEAD_DIM), st],
        out_specs=(pl.BlockSpec((chunk, w_b), lambda b, c: (b * n_chunks + c, 0)), st),
        scratch_shapes=[pltpu.VMEM((n_heads, B_HEAD_DIM, B_HEAD_DIM), F32)],
        compiler_params=_cparams(2),
        name="hgrn2",
    )(proj, proj, proj, proj, sum_mat, lb, og, s0)


def _gelu_tanh(x):
    return 0.5 * x * (1.0 + jnp.tanh(math.sqrt(2.0 / math.pi) * (x + 0.044715 * x * x * x)))


def _s5_kernel(u_ref, bre_ref, bim_ref, cre_ref, cim_ref, d_ref, tab_ref, x0re_ref, x0im_ref,
               glu_ref, og_ref, o_ref, fre_ref, fim_ref, xre_sc, xim_sc, cr_sc, ci_sc):
    t = pl.program_id(1)
    rows, w_c = u_ref.shape
    n_state = xre_sc.shape[1]
    n_blk = w_c // LANES
    sb = n_state // n_blk

    @pl.when(t == 0)
    def _():
        cr_sc[...] = jnp.broadcast_to(x0re_ref[0], cr_sc.shape)
        ci_sc[...] = jnp.broadcast_to(x0im_ref[0], ci_sc.shape)

    u = u_ref[...]
    for j in range(n_blk):
        uj = u[:, j * LANES:(j + 1) * LANES]
        xre_sc[:, j * sb:(j + 1) * sb] = _dot3(uj, bre_ref[j])
        xim_sc[:, j * sb:(j + 1) * sb] = _dot3(uj, bim_ref[j])

    half = n_state // 2

    def body(r, carry):
        r0 = pl.multiple_of(r * SUBLANES, SUBLANES)
        for hs in range(2):
            cs = slice(hs * half, (hs + 1) * half)
            xr = xre_sc[pl.ds(r0, SUBLANES), cs]
            xi = xim_sc[pl.ds(r0, SUBLANES), cs]
            for lev, shift in enumerate((1, 2, 4)):
                ar = tab_ref[2 * lev, :, cs]
                ai = tab_ref[2 * lev + 1, :, cs]
                sr = pltpu.roll(xr, shift, 0)
                si = pltpu.roll(xi, shift, 0)
                xr, xi = xr + (ar * sr - ai * si), xi + (ar * si + ai * sr)
            pr = tab_ref[6, :, cs]
            pi = tab_ref[7, :, cs]
            cr = cr_sc[:, cs]
            ci = ci_sc[:, cs]
            xr, xi = xr + (pr * cr - pi * ci), xi + (pr * ci + pi * cr)
            xre_sc[pl.ds(r0, SUBLANES), cs] = xr
            xim_sc[pl.ds(r0, SUBLANES), cs] = xi
            cr_sc[:, cs] = jnp.broadcast_to(xr[SUBLANES - 1:SUBLANES, :], (SUBLANES, half))
            ci_sc[:, cs] = jnp.broadcast_to(xi[SUBLANES - 1:SUBLANES, :], (SUBLANES, half))
        return carry

    lax.fori_loop(0, rows // SUBLANES, body, 0)

    ys = []
    for j in range(n_blk):
        ys.append(_dot3(xre_sc[:, j * sb:(j + 1) * sb], cre_ref[j])
                  - _dot3(xim_sc[:, j * sb:(j + 1) * sb], cim_ref[j]))
    yc = _gelu_tanh(jnp.concatenate(ys, axis=1) + d_ref[...] * u)
    gl = jnp.dot(yc.astype(BF16), glu_ref[...], preferred_element_type=F32)
    v = gl[:, :w_c] * _sigmoid(gl[:, w_c:])
    o_ref[...] = (_rms(v) * og_ref[...]).astype(o_ref.dtype)
    fre_ref[0] = cr_sc[0:1, :]
    fim_ref[0] = ci_sc[0:1, :]


def _s5(proj, wts, x0re, x0im, col0, row_blk0, n_batch, n_tiles, tile, w_c):
    bre, bim, cre, cim, dvec, tab, glu, og = wts
    n_state = tab.shape[2]
    cb = col0 // w_c
    full = lambda a: pl.BlockSpec(a.shape, lambda b, t, nd=a.ndim: (0,) * nd)
    st = pl.BlockSpec((1, 1, n_state), lambda b, t: (b, 0, 0))
    return pl.pallas_call(
        _s5_kernel,
        out_shape=(jax.ShapeDtypeStruct((n_batch * n_tiles * tile, w_c), BF16),
                   jax.ShapeDtypeStruct((n_batch, 1, n_state), F32),
                   jax.ShapeDtypeStruct((n_batch, 1, n_state), F32)),
        grid=(n_batch, n_tiles),
        in_specs=[pl.BlockSpec((tile, w_c), lambda b, t: (row_blk0 + b * n_tiles + t, cb)),
                  full(bre), full(bim), full(cre), full(cim), full(dvec), full(tab), st, st,
                  full(glu), full(og)],
        out_specs=(pl.BlockSpec((tile, w_c), lambda b, t: (b * n_tiles + t, 0)), st, st),
        scratch_shapes=[pltpu.VMEM((tile, n_state), F32), pltpu.VMEM((tile, n_state), F32),
                        pltpu.VMEM((SUBLANES, n_state), F32), pltpu.VMEM((SUBLANES, n_state), F32)],
        compiler_params=_cparams(2),
        name="s5",
    )(proj, bre, bim, cre, cim, dvec, tab, x0re, x0im, glu, og)


def _s5_weights(a_re, a_im, b_re, b_im, c_re, c_im, d, log_dt, glu_w, out_g):
    g, n = a_re.shape
    p = b_re.shape[2]
    lam_re = jnp.minimum(a_re.astype(F32), A_RE_MAX)
    lam_im = a_im.astype(F32)
    dt = jnp.exp(log_dt.astype(F32))[:, None]
    mag = jnp.exp(lam_re * dt)
    abar_re = mag * jnp.cos(lam_im * dt)
    abar_im = mag * jnp.sin(lam_im * dt)
    den = lam_re * lam_re + lam_im * lam_im
    z_re = abar_re - 1.0
    coef_re = (z_re * lam_re + abar_im * lam_im) / den
    coef_im = (abar_im * lam_re - z_re * lam_im) / den
    b_re = b_re.astype(F32)
    b_im = b_im.astype(F32)
    bbar_re = coef_re[..., None] * b_re - coef_im[..., None] * b_im
    bbar_im = coef_re[..., None] * b_im + coef_im[..., None] * b_re

    gpb = LANES // p
    n_blk = g // gpb
    eye = jnp.eye(gpb, dtype=F32)

    def blockdiag_in(bb):
        x = bb.reshape(n_blk, gpb, n, p)
        return jnp.einsum('jgnp,gh->jgphn', x, eye).reshape(n_blk, gpb * p, gpb * n)

    def blockdiag_out(cc):
        x = cc.astype(F32).reshape(n_blk, gpb, p, n)
        return jnp.einsum('jgpn,gh->jgnhp', x, eye).reshape(n_blk, gpb * n, gpb * p)

    pw_re, pw_im = [abar_re.reshape(-1)], [abar_im.reshape(-1)]
    for _ in range(SUBLANES - 1):
        pr, pi = pw_re[-1], pw_im[-1]
        pw_re.append(pr * pw_re[0] - pi * pw_im[0])
        pw_im.append(pr * pw_im[0] + pi * pw_re[0])
    rows = jnp.arange(SUBLANES)[:, None]
    tabs = []
    for k in (1, 2, 4):
        keep = rows >= k
        tabs.append(jnp.where(keep, pw_re[k - 1][None, :], 0.0))
        tabs.append(jnp.where(keep, pw_im[k - 1][None, :], 0.0))
    tabs.append(jnp.stack(pw_re))
    tabs.append(jnp.stack(pw_im))
    tab = jnp.stack(tabs)
    return (blockdiag_in(bbar_re), blockdiag_in(bbar_im), blockdiag_out(c_re), blockdiag_out(c_im),
            d.astype(F32).reshape(1, g * p), tab, glu_w.astype(BF16), out_g.astype(F32).reshape(1, -1))


def _take_rows(a, idx):
    return a.at[idx].get(mode="promise_in_bounds")


def _moe_plan(route, n_experts, m_pad):
    n_tok = route.shape[0]
    n = TOP_K * n_tok
    tm = MOE_ROW_TILE
    e_flat = jnp.concatenate([route[:, 0], route[:, 1]]).astype(jnp.int32)
    onehot = (e_flat[:, None] == jnp.arange(n_experts, dtype=jnp.int32)[None, :]).astype(jnp.int32)
    csum = jnp.cumsum(onehot, axis=0)
    counts = csum[-1]
    rank = jnp.sum(csum * onehot, axis=1) - 1
    padded = (counts + tm - 1) // tm * tm
    ends_pad = jnp.cumsum(padded)
    start_pad = ends_pad - padded
    start = jnp.cumsum(counts) - counts
    pos = jnp.sum(onehot * start_pad[None, :], axis=1) + rank
    order = jnp.argsort(e_flat, stable=True)
    n_tiles = m_pad // tm
    tile_start = jnp.arange(n_tiles, dtype=jnp.int32) * tm
    tile_expert = jnp.sum((tile_start[:, None] >= ends_pad[None, :]).astype(jnp.int32), axis=1)
    tile_expert = jnp.minimum(tile_expert, jnp.max(e_flat))
    k = jnp.arange(m_pad, dtype=jnp.int32) - jnp.repeat(start_pad[tile_expert], tm)
    valid = k < jnp.repeat(counts[tile_expert], tm)
    slot = jnp.clip(jnp.repeat(start[tile_expert], tm) + k, 0, n - 1)
    row_token = jnp.where(valid, _take_rows(order, slot) % n_tok, 0)
    n_valid = (ends_pad[-1] // tm).astype(jnp.int32)
    return row_token, pos[:n_tok], pos[n_tok:], tile_expert, n_valid.reshape(1)


def _cast_pad_kernel(w_ref, o_ref):
    n = w_ref.shape[1]
    o_ref[:, :n] = w_ref[...].astype(o_ref.dtype)
    if o_ref.shape[1] > n:
        o_ref[:, n:] = jnp.zeros((o_ref.shape[0], o_ref.shape[1] - n), o_ref.dtype)


def _cast_pad_cols(w, n_pad):
    lead, n = w.shape[:-1], w.shape[-1]
    rows = math.prod(lead)
    tr = MOE_ROW_TILE
    out = pl.pallas_call(
        _cast_pad_kernel,
        out_shape=jax.ShapeDtypeStruct((rows, n_pad), BF16),
        grid=(rows // tr,),
        in_specs=[pl.BlockSpec((tr, n), lambda i: (i, 0))],
        out_specs=pl.BlockSpec((tr, n_pad), lambda i: (i, 0)),
        compiler_params=_cparams(1),
        name="cast_pad",
    )(w.reshape(rows, n))
    return out.reshape(lead + (n_pad,))


def kernel(x_prompt, x_sample, c_prompt, c_sample, cache_k, cache_v, state_hgrn, state_ssm_re, state_ssm_im, w_ada, b_ada, norm1_g, norm2_g, w_in, w_out, q_norm_g, k_norm_g, lambda_qk, attn_out_g, hgrn_lower_bounds, hgrn_out_g, ssm_a_re, ssm_a_im, ssm_b_re, ssm_b_im, ssm_c_re, ssm_c_im, ssm_d, ssm_log_dt, ssm_glu_w, ssm_out_g, ffn_w1, ffn_w3, ffn_w2, moe_router, moe_w1, moe_w3, moe_w2):
    bp, t, d = x_prompt.shape
    bs, ts, _ = x_sample.shape
    depth = w_in.shape[0]
    n_past = cache_k.shape[2]
    n_heads = cache_k.shape[3]
    w_a = n_heads * A_VDIM
    w_b = hgrn_lower_bounds.shape[1]
    w_c = ssm_out_g.shape[1]
    n_state = ssm_a_re.shape[1] * ssm_a_re.shape[2]
    n_sample = bs * ts
    assert bp == 1 and t % ROW_TILE == 0 and t % ATTN_TILE == 0 and t % ATTN_FAST_TILE == 0
    assert t % S5_TILE == 0 and t % HGRN_TILE == 0
    assert n_sample % ROW_TILE == 0 and ROW_TILE % ts == 0 and ts <= CHUNK and ts % 16 == 0
    assert (ts & (ts - 1)) == 0 and w_a == 2 * w_b and w_b == w_c
    n_pt = t // ROW_TILE
    m_tot = t + n_sample

    x = jnp.concatenate([x_prompt.reshape(t, d), x_sample.reshape(n_sample, d)], axis=0)

    c_rows = jnp.concatenate([c_prompt, c_sample], axis=0)
    c_rows = jnp.pad(c_rows, ((0, _round_up(1 + bs, 16) - (1 + bs)), (0, 0)))
    mod = _modulation(c_rows.astype(F32), w_ada, b_ada)

    def mod_blocks(l, k):
        m = mod[l, :, k * d:(k + 1) * d]
        return jnp.concatenate([jnp.broadcast_to(m[0:1], (ROW_TILE, d)),
                                jnp.repeat(m[1:1 + bs], ts, axis=0)], axis=0)

    lb_p = jax.nn.softmax(hgrn_lower_bounds.astype(F32), axis=0)
    lower_bounds = jnp.cumsum(lb_p, axis=0) - lb_p[0]

    zero_hgrn = jnp.zeros((bp,) + state_hgrn.shape[2:], F32)
    zero_ssm = jnp.zeros((bp, 1, n_state), F32)
    n_keys = n_past + ts
    keys_pad = _round_up(n_keys, LANES)

    kp, vp, sp, rep, imp = [], [], [], [], []
    ks, vs, ss, res, ims = [], [], [], [], []
    for l in range(depth):
        sh1, sc1, g1, sh2, sc2, g2 = [mod_blocks(l, k) for k in range(6)]
        h = _norm_mod(x, norm1_g[l].reshape(1, d), sc1, sh1, n_pt)
        proj = _matmul(h, w_in[l].astype(BF16))

        lam_init = 0.8 - 0.6 * math.exp(-0.3 * l)
        lq = lambda_qk[l].astype(F32)
        lam = jnp.exp(jnp.sum(lq[0] * lq[1])) - jnp.exp(jnp.sum(lq[2] * lq[3])) + lam_init
        lam_vec = jnp.full((1, LANES), lam, F32)
        og_a = (attn_out_g[l].astype(F32) * (1.0 - lam_init)).reshape(1, A_VDIM)
        gq = jnp.tile(q_norm_g[l].astype(F32), 2).reshape(1, LANES)
        gk = jnp.tile(k_norm_g[l].astype(F32), 2).reshape(1, LANES)
        qn, kn, kb, vb = _qknorm(proj, gq, gk, w_a)
        score_bound = 1.02 * A_HEAD_DIM * Q_SCALE * jnp.max(jnp.abs(gq)) * jnp.max(jnp.abs(gk))
        oa_p = _attn_prompt(qn, kb, vb, score_bound, og_a, lam_vec, t, n_heads)
        pad = jnp.zeros((bs, keys_pad - n_keys, w_a), BF16)
        k_all = jnp.concatenate([cache_k[l].reshape(bs, n_past, w_a).astype(BF16),
                                 kb[t:].reshape(bs, ts, w_a), pad], axis=1)
        v_all = jnp.concatenate([cache_v[l].reshape(bs, n_past, w_a).astype(BF16),
                                 vb[t:].reshape(bs, ts, w_a), pad], axis=1)
        oa_s = _attn_sample(qn, k_all, v_all, og_a, lam_vec, t, bs, ts, n_heads, n_past, n_keys)

        lb = lower_bounds[l].reshape(1, w_b)
        og_b = hgrn_out_g[l].astype(F32).reshape(1, B_HEAD_DIM)
        col_b = 3 * w_a
        ob_p, s_p = _hgrn(proj, lb, og_b, zero_hgrn, col_b, 0, bp, t // HGRN_TILE, HGRN_TILE, w_b)
        ob_s, s_s = _hgrn(proj, lb, og_b, state_hgrn[l].astype(F32), col_b, t // ts, bs, 1, ts, w_b)

        wts = _s5_weights(ssm_a_re[l], ssm_a_im[l], ssm_b_re[l], ssm_b_im[l], ssm_c_re[l], ssm_c_im[l],
                          ssm_d[l], ssm_log_dt[l], ssm_glu_w[l], ssm_out_g[l])
        col_c = 3 * w_a + 4 * w_b
        oc_p, re_p, im_p = _s5(proj, wts, zero_ssm, zero_ssm, col_c, 0, bp, t // S5_TILE, S5_TILE, w_c)
        oc_s, re_s, im_s = _s5(proj, wts, state_ssm_re[l].astype(F32).reshape(bs, 1, n_state),
                               state_ssm_im[l].astype(F32).reshape(bs, 1, n_state),
                               col_c, t // ts, bs, 1, ts, w_c)

        merged = jnp.concatenate([jnp.concatenate([oa_p, ob_p, oc_p], axis=1),
                                  jnp.concatenate([oa_s, ob_s, oc_s], axis=1)], axis=0)
        x = _matmul_residual(merged, w_out[l].astype(BF16), x, g1, n_pt)

        if l % 2 == 0:
            j = l // 2
            dff = ffn_w1.shape[2]
            dff_pad = _round_up(dff, 512)
            h2 = _norm_mod(x, norm2_g[l].reshape(1, d), sc2, sh2, n_pt)
            a = _swiglu_up(h2, _cast_pad_cols(ffn_w1[j], dff_pad), _cast_pad_cols(ffn_w3[j], dff_pad))
            x = _matmul_residual(a, ffn_w2[j].astype(BF16), x, g2, n_pt)
        else:
            j = l // 2
            n_experts = moe_router.shape[2]
            dff = moe_w1.shape[3]
            dff_pad = _round_up(dff, 512)
            h2, route = _norm_mod(x, norm2_g[l].reshape(1, d), sc2, sh2, n_pt, w_router=moe_router[j])
            m_pad = TOP_K * m_tot + n_experts * MOE_ROW_TILE
            row_token, pos0, pos1, tile_expert, n_valid = _moe_plan(route, n_experts, m_pad)
            up = _grouped_swiglu_up(_take_rows(h2, row_token), _cast_pad_cols(moe_w1[j], dff_pad),
                                    _cast_pad_cols(moe_w3[j], dff_pad), tile_expert, n_valid)
            down = _grouped_down(up, moe_w2[j].astype(BF16), tile_expert, n_valid)
            x = _moe_combine(x, _take_rows(down, pos0), _take_rows(down, pos1), route, g2, n_pt)

        v_all_tokens = proj[:, 2 * w_a:3 * w_a]
        kp.append(kn[:t].reshape(bp, t, n_heads, 2, A_HEAD_DIM))
        vp.append(v_all_tokens[:t].reshape(bp, t, n_heads, A_VDIM))
        sp.append(s_p)
        rep.append(re_p.reshape(bp, -1, C_STATE))
        imp.append(im_p.reshape(bp, -1, C_STATE))
        ks.append(kn[t:].reshape(bs, ts, n_heads, 2, A_HEAD_DIM))
        vs.append(v_all_tokens[t:].reshape(bs, ts, n_heads, A_VDIM))
        ss.append(s_s)
        res.append(re_s.reshape(bs, -1, C_STATE))
        ims.append(im_s.reshape(bs, -1, C_STATE))

    return (x[:t].reshape(bp, t, d), x[t:].reshape(bs, ts, d),
            jnp.stack(kp), jnp.stack(vp), jnp.stack(sp), jnp.stack(rep), jnp.stack(imp),
            jnp.stack(ks), jnp.stack(vs), jnp.stack(ss), jnp.stack(res), jnp.stack(ims))
```

```python
import functools
import math

import numpy as np
import jax
import jax.numpy as jnp
from jax import lax
from jax.experimental import pallas as pl
from jax.experimental.pallas import tpu as pltpu

F32 = jnp.float32
BF16 = jnp.bfloat16

CHUNK = 64
A_HEAD_DIM = 64
A_VDIM = 2 * A_HEAD_DIM
B_HEAD_DIM = 128
C_GROUP = 16
C_STATE = 64
TOP_K = 2
RMS_EPS = 1e-6
NEG_INF = -1e30
A_RE_MAX = -1e-4

LANES = 128
SUBLANES = 8
ROW_TILE = 512
MOE_ROW_TILE = 256
ATTN_TILE = 512
ATTN_FAST_TILE = 1024
ATTN_DIAG_SUB = 256
ATTN_SHIFT_MAX = 60.0
HGRN_TILE = 256
S5_TILE = 512
VMEM_LIMIT = 48 * 1024 * 1024
CAST_BLOCK_BYTES = 6 * 1024 * 1024

Q_SCALE = (A_HEAD_DIM ** -0.5) * math.log2(math.e)

NT_DIMS = (((1,), (1,)), ((), ()))
TN_DIMS = (((0,), (0,)), ((), ()))
NN_DIMS = (((1,), (0,)), ((), ()))


def _cparams(n_axes):
    return pltpu.CompilerParams(
        dimension_semantics=("arbitrary",) * n_axes, vmem_limit_bytes=VMEM_LIMIT)


def _round_up(x, m):
    return (x + m - 1) // m * m


def _pick_tile(n, target):
    best = LANES
    for t in range(LANES, min(n, target) + 1, LANES):
        if n % t == 0:
            best = t
    return best


def _split(x):
    hi = x.astype(BF16)
    lo = (x - hi.astype(F32)).astype(BF16)
    return hi, lo


def _dot3(a, b, dims=NN_DIMS):
    ah, al = _split(a)
    bh, bl = _split(b)
    d = functools.partial(lax.dot_general, dimension_numbers=dims, preferred_element_type=F32)
    return d(ah, bh) + d(al, bh) + d(ah, bl)


def _dot2(a, w):
    ah, al = _split(a)
    return jnp.dot(ah, w, preferred_element_type=F32) + jnp.dot(al, w, preferred_element_type=F32)


def _sigmoid(x):
    return 1.0 / (1.0 + jnp.exp(-x))


def _silu(x):
    return x * _sigmoid(x)


def _rms(x):
    return x * lax.rsqrt(jnp.mean(x * x, axis=-1, keepdims=True) + RMS_EPS)


def _mod_kernel(c_ref, w_ref, b_ref, o_ref):
    o_ref[0] = _dot3(_silu(c_ref[...]), w_ref[0]) + b_ref[0]


def _modulation(c_rows, w_ada, b_ada):
    depth, d, n = w_ada.shape
    rows = c_rows.shape[0]
    tn = _pick_tile(n, 1024)
    return pl.pallas_call(
        _mod_kernel,
        out_shape=jax.ShapeDtypeStruct((depth, rows, n), F32),
        grid=(depth, n // tn),
        in_specs=[
            pl.BlockSpec((rows, d), lambda l, j: (0, 0)),
            pl.BlockSpec((1, d, tn), lambda l, j: (l, 0, j)),
            pl.BlockSpec((1, 1, tn), lambda l, j: (l, 0, j)),
        ],
        out_specs=pl.BlockSpec((1, rows, tn), lambda l, j: (l, 0, j)),
        compiler_params=_cparams(2),
        name="adaln_mod",
    )(c_rows, w_ada, b_ada.reshape(depth, 1, n))


def _norm_mod_kernel(x_ref, g_ref, sc_ref, sh_ref, h_ref):
    y = _rms(x_ref[...]) * g_ref[...]
    h_ref[...] = (y * (1.0 + sc_ref[...]) + sh_ref[...]).astype(h_ref.dtype)


def _norm_mod_route_kernel(x_ref, g_ref, sc_ref, sh_ref, wr_ref, h_ref, r_ref, *, n_experts):
    y = _rms(x_ref[...]) * g_ref[...]
    h = y * (1.0 + sc_ref[...]) + sh_ref[...]
    h_ref[...] = h.astype(h_ref.dtype)
    logits = _dot3(h, wr_ref[...])
    lane = lax.broadcasted_iota(jnp.int32, logits.shape, 1).astype(F32)
    lg = jnp.where(lane < n_experts, logits, -jnp.inf)
    m1 = jnp.max(lg, axis=1, keepdims=True)
    i1 = jnp.min(jnp.where(lg == m1, lane, float(LANES)), axis=1, keepdims=True)
    lg2 = jnp.where(lane == i1, -jnp.inf, lg)
    m2 = jnp.max(lg2, axis=1, keepdims=True)
    i2 = jnp.min(jnp.where(lg2 == m2, lane, float(LANES)), axis=1, keepdims=True)
    e = jnp.exp(m2 - m1)
    inv = 1.0 / (1.0 + e)
    r_ref[...] = jnp.where(lane == 0.0, i1, jnp.where(lane == 1.0, i2, jnp.where(
        lane == 2.0, inv, jnp.where(lane == 3.0, e * inv, 0.0))))


def _mod_index(n_prompt_tiles):
    return lambda i: (jnp.maximum(i - n_prompt_tiles + 1, 0), 0)


def _norm_mod(x, g, sc, sh, n_prompt_tiles, w_router=None):
    m, d = x.shape
    mi = _mod_index(n_prompt_tiles)
    in_specs = [
        pl.BlockSpec((ROW_TILE, d), lambda i: (i, 0)),
        pl.BlockSpec((1, d), lambda i: (0, 0)),
        pl.BlockSpec((ROW_TILE, d), mi),
        pl.BlockSpec((ROW_TILE, d), mi),
    ]
    h_spec = pl.BlockSpec((ROW_TILE, d), lambda i: (i, 0))
    if w_router is None:
        return pl.pallas_call(
            _norm_mod_kernel,
            out_shape=jax.ShapeDtypeStruct((m, d), BF16),
            grid=(m // ROW_TILE,),
            in_specs=in_specs,
            out_specs=h_spec,
            compiler_params=_cparams(1),
            name="norm_mod",
        )(x, g, sc, sh)
    n_experts = w_router.shape[1]
    wr = jnp.zeros((d, LANES), F32).at[:, :n_experts].set(w_router.astype(F32))
    return pl.pallas_call(
        functools.partial(_norm_mod_route_kernel, n_experts=n_experts),
        out_shape=(jax.ShapeDtypeStruct((m, d), BF16), jax.ShapeDtypeStruct((m, LANES), F32)),
        grid=(m // ROW_TILE,),
        in_specs=in_specs + [pl.BlockSpec((d, LANES), lambda i: (0, 0))],
        out_specs=(h_spec, pl.BlockSpec((ROW_TILE, LANES), lambda i: (i, 0))),
        compiler_params=_cparams(1),
        name="norm_mod_route",
    )(x, g, sc, sh, wr)


def _mm_kernel(a_ref, w_ref, o_ref):
    o_ref[...] = jnp.dot(a_ref[...], w_ref[...], preferred_element_type=F32).astype(o_ref.dtype)


def _mm_res_kernel(*refs, widths):
    a_refs, (w_ref, x_ref, g_ref, o_ref) = refs[:len(widths)], refs[len(widths):]
    acc, k0 = None, 0
    for a_ref, kw in zip(a_refs, widths):
        part = jnp.dot(a_ref[...], w_ref[k0:k0 + kw, :], preferred_element_type=F32)
        acc = part if acc is None else acc + part
        k0 += kw
    o_ref[...] = x_ref[...] + g_ref[...] * acc


def _swiglu_up_kernel(a_ref, w1_ref, w3_ref, o_ref):
    a = a_ref[...]
    u = jnp.dot(a, w1_ref[...], preferred_element_type=F32)
    v = jnp.dot(a, w3_ref[...], preferred_element_type=F32)
    o_ref[...] = (_silu(u) * v).astype(o_ref.dtype)


def _matmul(a, w, tn_target=1408):
    m, k = a.shape
    n = w.shape[1]
    tn = _pick_tile(n, tn_target)
    return pl.pallas_call(
        _mm_kernel,
        out_shape=jax.ShapeDtypeStruct((m, n), F32),
        grid=(n // tn, m // ROW_TILE),
        in_specs=[
            pl.BlockSpec((ROW_TILE, k), lambda j, i: (i, 0)),
            pl.BlockSpec((k, tn), lambda j, i: (0, j)),
        ],
        out_specs=pl.BlockSpec((ROW_TILE, tn), lambda j, i: (i, j)),
        compiler_params=_cparams(2),
        name="matmul",
    )(a, w)


def _matmul_residual(a, w, x, gate, n_prompt_tiles, tn_target=512):
    k, n = w.shape
    parts = a if isinstance(a, tuple) else (a,)
    widths = tuple(p.shape[1] for p in parts) if isinstance(a, tuple) else (k,)
    assert sum(widths) == k
    m = parts[0].shape[0]
    tn = _pick_tile(n, tn_target)
    return pl.pallas_call(
        functools.partial(_mm_res_kernel, widths=widths),
        out_shape=jax.ShapeDtypeStruct((m, n), F32),
        grid=(n // tn, m // ROW_TILE),
        in_specs=[pl.BlockSpec((ROW_TILE, kw), lambda j, i: (i, 0)) for kw in widths] + [
            pl.BlockSpec((k, tn), lambda j, i: (0, j)),
            pl.BlockSpec((ROW_TILE, tn), lambda j, i: (i, j)),
            pl.BlockSpec((ROW_TILE, tn), lambda j, i: (jnp.maximum(i - n_prompt_tiles + 1, 0), j)),
        ],
        out_specs=pl.BlockSpec((ROW_TILE, tn), lambda j, i: (i, j)),
        compiler_params=_cparams(2),
        name="matmul_residual",
    )(*parts, w, x, gate)


def _swiglu_up(a, w1, w3, tn_target=1408):
    m, k = a.shape
    n = w1.shape[1]
    tn = _pick_tile(n, tn_target)
    return pl.pallas_call(
        _swiglu_up_kernel,
        out_shape=jax.ShapeDtypeStruct((m, n), BF16),
        grid=(n // tn, m // ROW_TILE),
        in_specs=[
            pl.BlockSpec((ROW_TILE, k), lambda j, i: (i, 0)),
            pl.BlockSpec((k, tn), lambda j, i: (0, j)),
            pl.BlockSpec((k, tn), lambda j, i: (0, j)),
        ],
        out_specs=pl.BlockSpec((ROW_TILE, tn), lambda j, i: (i, j)),
        compiler_params=_cparams(2),
        name="swiglu_up",
    )(a, w1, w3)


def _g_up_kernel(te_ref, nv_ref, a_ref, w1_ref, w3_ref, o_ref):
    i = pl.program_id(1)

    @pl.when(i < nv_ref[0])
    def _():
        a = a_ref[...]
        u = jnp.dot(a, w1_ref[0], preferred_element_type=F32)
        v = jnp.dot(a, w3_ref[0], preferred_element_type=F32)
        o_ref[...] = (_silu(u) * v).astype(o_ref.dtype)

    @pl.when(i >= nv_ref[0])
    def _():
        o_ref[...] = jnp.zeros_like(o_ref)


def _g_down_kernel(te_ref, nv_ref, a_ref, w_ref, o_ref):
    i = pl.program_id(1)

    @pl.when(i < nv_ref[0])
    def _():
        o_ref[...] = jnp.dot(a_ref[...], w_ref[0], preferred_element_type=F32)

    @pl.when(i >= nv_ref[0])
    def _():
        o_ref[...] = jnp.zeros_like(o_ref)


def _grouped_swiglu_up(a, w1, w3, tile_expert, n_valid, tn_target=1408):
    m, k = a.shape
    n = w1.shape[2]
    tn = _pick_tile(n, tn_target)
    tm = MOE_ROW_TILE
    grid_spec = pltpu.PrefetchScalarGridSpec(
        num_scalar_prefetch=2,
        grid=(n // tn, m // tm),
        in_specs=[
            pl.BlockSpec((tm, k), lambda j, i, te, nv: (i, 0)),
            pl.BlockSpec((1, k, tn), lambda j, i, te, nv: (te[i], 0, j)),
            pl.BlockSpec((1, k, tn), lambda j, i, te, nv: (te[i], 0, j)),
        ],
        out_specs=pl.BlockSpec((tm, tn), lambda j, i, te, nv: (i, j)),
    )
    return pl.pallas_call(
        _g_up_kernel,
        out_shape=jax.ShapeDtypeStruct((m, n), BF16),
        grid_spec=grid_spec,
        compiler_params=_cparams(2),
        name="moe_swiglu_up",
    )(tile_expert, n_valid, a, w1, w3)


def _grouped_down(a, w, tile_expert, n_valid, tn_target=512):
    m = a.shape[0]
    _, k, n = w.shape
    tn = _pick_tile(n, tn_target)
    tm = MOE_ROW_TILE
    grid_spec = pltpu.PrefetchScalarGridSpec(
        num_scalar_prefetch=2,
        grid=(n // tn, m // tm),
        in_specs=[
            pl.BlockSpec((tm, k), lambda j, i, te, nv: (i, 0)),
            pl.BlockSpec((1, k, tn), lambda j, i, te, nv: (te[i], 0, j)),
        ],
        out_specs=pl.BlockSpec((tm, tn), lambda j, i, te, nv: (i, j)),
    )
    return pl.pallas_call(
        _g_down_kernel,
        out_shape=jax.ShapeDtypeStruct((m, n), F32),
        grid_spec=grid_spec,
        compiler_params=_cparams(2),
        name="moe_down",
    )(tile_expert, n_valid, a, w)


def _moe_combine_kernel(x_ref, r0_ref, r1_ref, rt_ref, g_ref, o_ref):
    rt = rt_ref[...]
    y = rt[:, 2:3] * r0_ref[...] + rt[:, 3:4] * r1_ref[...]
    o_ref[...] = x_ref[...] + g_ref[...] * y


def _moe_combine(x, r0, r1, route, gate, n_prompt_tiles):
    m, d = x.shape
    row = pl.BlockSpec((ROW_TILE, d), lambda i: (i, 0))
    return pl.pallas_call(
        _moe_combine_kernel,
        out_shape=jax.ShapeDtypeStruct((m, d), F32),
        grid=(m // ROW_TILE,),
        in_specs=[row, row, row, pl.BlockSpec((ROW_TILE, LANES), lambda i: (i, 0)),
                  pl.BlockSpec((ROW_TILE, d), _mod_index(n_prompt_tiles))],
        out_specs=row,
        compiler_params=_cparams(1),
        name="moe_combine",
    )(x, r0, r1, route, gate)


def _qknorm_kernel(q_ref, k_ref, v_ref, gq_ref, gk_ref, qn_ref, kn_ref, kb_ref, vb_ref):
    r = lax.broadcasted_iota(jnp.int32, (LANES, LANES), 0) // A_HEAD_DIM
    c = lax.broadcasted_iota(jnp.int32, (LANES, LANES), 1) // A_HEAD_DIM
    seg = jnp.where(r == c, 1.0 / A_HEAD_DIM, 0.0).astype(BF16)

    def norm(x, g):
        hi, lo = _split(x * x)
        ms = (jnp.dot(hi, seg, preferred_element_type=F32)
              + jnp.dot(lo, seg, preferred_element_type=F32))
        return x * lax.rsqrt(ms + RMS_EPS) * g

    for h in range(q_ref.shape[1] // LANES):
        sl = slice(h * LANES, (h + 1) * LANES)
        qn = norm(q_ref[:, sl], gq_ref[...])
        qn_ref[:, sl] = (qn * Q_SCALE).astype(BF16)
        kn = norm(k_ref[:, sl], gk_ref[...])
        kn_ref[:, sl] = kn
        kb_ref[:, sl] = kn.astype(BF16)
    vb_ref[...] = v_ref[...].astype(BF16)


def _qknorm(proj, gq, gk, w_a):
    m = proj.shape[0]
    blk = lambda c: pl.BlockSpec((ROW_TILE, w_a), lambda i, c=c: (i, c))
    g_spec = pl.BlockSpec((1, LANES), lambda i: (0, 0))
    out = pl.BlockSpec((ROW_TILE, w_a), lambda i: (i, 0))
    return pl.pallas_call(
        _qknorm_kernel,
        out_shape=(jax.ShapeDtypeStruct((m, w_a), BF16), jax.ShapeDtypeStruct((m, w_a), F32),
                   jax.ShapeDtypeStruct((m, w_a), BF16), jax.ShapeDtypeStruct((m, w_a), BF16)),
        grid=(m // ROW_TILE,),
        in_specs=[blk(0), blk(1), blk(2), g_spec, g_spec],
        out_specs=(out, out, out, out),
        compiler_params=_cparams(1),
        name="qk_norm",
    )(proj, proj, proj, gq, gk)


def _stack_maps(q):
    lane = lax.broadcasted_iota(jnp.int32, q.shape, 1)
    zero = jnp.zeros_like(q)
    return jnp.where(lane < A_HEAD_DIM, q, zero), jnp.where(lane >= A_HEAD_DIM, q, zero)


def _attn_finish(acc, l, tq, lam, g):
    o = acc[:tq] / l[:tq] - lam * (acc[tq:] / l[tq:])
    return _rms(o) * g


def _attn_prompt_kernel(q_ref, k_ref, v_ref, g_ref, lam_ref, o_ref, qs_sc, m_sc, l_sc, acc_sc, *, tq):
    i = pl.program_id(1)
    q1, q2 = _stack_maps(q_ref[...])
    qs_sc[0:tq, :] = q1
    qs_sc[tq:2 * tq, :] = q2
    m_sc[...] = jnp.full(m_sc.shape, NEG_INF, F32)
    l_sc[...] = jnp.zeros(l_sc.shape, F32)
    acc_sc[...] = jnp.zeros(acc_sc.shape, F32)

    def step(j, masked):
        start = pl.multiple_of(j * tq, tq)
        kb = k_ref[pl.ds(start, tq), :]
        vb = v_ref[pl.ds(start, tq), :]
        s = lax.dot_general(qs_sc[...], kb, NT_DIMS, preferred_element_type=F32)
        if masked:
            row = lax.broadcasted_iota(jnp.int32, s.shape, 0)
            col = lax.broadcasted_iota(jnp.int32, s.shape, 1)
            s = jnp.where(col // CHUNK <= (row % tq) // CHUNK, s, NEG_INF)
        m_prev = m_sc[...]
        m_next = jnp.maximum(m_prev, jnp.max(s, axis=1, keepdims=True))
        p = jnp.exp2(s - m_next[:, 0:1])
        alpha = jnp.exp2(m_prev - m_next)
        l_sc[...] = alpha * l_sc[...] + jnp.sum(p, axis=1, keepdims=True)
        acc_sc[...] = alpha * acc_sc[...] + jnp.dot(p.astype(BF16), vb, preferred_element_type=F32)
        m_sc[...] = m_next

    def body(j, carry):
        step(j, False)
        return carry

    lax.fori_loop(0, i, body, 0)
    step(i, True)
    o_ref[...] = _attn_finish(acc_sc[...], l_sc[...], tq, lam_ref[...], g_ref[...]).astype(o_ref.dtype)


def _attn_prompt_shift_kernel(q_ref, k_ref, v_ref, sh_ref, g_ref, lam_ref, o_ref, qs_sc, acc_sc, *, tq):
    i = pl.program_id(1)
    q1, q2 = _stack_maps(q_ref[...])
    qs_sc[0:tq, :] = q1
    qs_sc[tq:2 * tq, :] = q2
    acc_sc[...] = jnp.zeros(acc_sc.shape, F32)
    shift = sh_ref[:, 0:1]
    ones = jnp.ones((tq, LANES), BF16)

    def body(j, carry):
        start = pl.multiple_of(j * tq, tq)
        kb = k_ref[pl.ds(start, tq), :]
        vb1 = jnp.concatenate([v_ref[pl.ds(start, tq), :], ones], axis=1)
        s = lax.dot_general(qs_sc[...], kb, NT_DIMS, preferred_element_type=F32)
        p = jnp.exp2(s - shift).astype(BF16)
        acc_sc[...] += jnp.dot(p, vb1, preferred_element_type=F32)
        return carry

    lax.fori_loop(0, i, body, 0)

    sub = ATTN_DIAG_SUB
    diag0 = pl.multiple_of(i * tq, tq)
    for j in range(tq // sub):
        k0 = pl.multiple_of(diag0 + j * sub, sub)
        kb = k_ref[pl.ds(k0, sub), :]
        vb1 = jnp.concatenate([v_ref[pl.ds(k0, sub), :], ones[:sub]], axis=1)
        for half in range(2):
            rows = slice(half * tq + j * sub, (half + 1) * tq)
            s = lax.dot_general(qs_sc[rows, :], kb, NT_DIMS, preferred_element_type=F32)
            row = lax.broadcasted_iota(jnp.int32, s.shape, 0)
            col = lax.broadcasted_iota(jnp.int32, s.shape, 1)
            s = jnp.where(col // CHUNK <= row // CHUNK, s, NEG_INF)
            p = jnp.exp2(s - shift).astype(BF16)
            acc_sc[rows, :] += jnp.dot(p, vb1, preferred_element_type=F32)

    acc = acc_sc[...]
    o_ref[...] = _attn_finish(acc[:, :A_VDIM], acc[:, A_VDIM:], tq, lam_ref[...], g_ref[...]).astype(o_ref.dtype)


def _attn_prompt_running_max(qn, kb, vb, g, lam, t, n_heads, out_rows):
    tq = ATTN_TILE
    kv_spec = pl.BlockSpec((t, A_VDIM), lambda h, i: (0, h))
    vec = pl.BlockSpec((1, LANES), lambda h, i: (0, 0))
    return pl.pallas_call(
        functools.partial(_attn_prompt_kernel, tq=tq),
        out_shape=jax.ShapeDtypeStruct((out_rows, n_heads * A_VDIM), BF16),
        grid=(n_heads, t // tq),
        in_specs=[pl.BlockSpec((tq, A_VDIM), lambda h, i: (i, h)), kv_spec, kv_spec, vec, vec],
        out_specs=pl.BlockSpec((tq, A_VDIM), lambda h, i: (i, h)),
        scratch_shapes=[pltpu.VMEM((2 * tq, A_VDIM), BF16), pltpu.VMEM((2 * tq, LANES), F32),
                        pltpu.VMEM((2 * tq, LANES), F32), pltpu.VMEM((2 * tq, A_VDIM), F32)],
        compiler_params=_cparams(2),
        name="attn_prompt",
    )(qn, kb, vb, g, lam)


def _attn_prompt_shift(qn, kb, vb, shift, g, lam, t, n_heads, out_rows):
    tq = ATTN_FAST_TILE
    kv_spec = pl.BlockSpec((t, A_VDIM), lambda h, i: (0, h))
    vec = pl.BlockSpec((1, LANES), lambda h, i: (0, 0))
    return pl.pallas_call(
        functools.partial(_attn_prompt_shift_kernel, tq=tq),
        out_shape=jax.ShapeDtypeStruct((out_rows, n_heads * A_VDIM), BF16),
        grid=(n_heads, t // tq),
        in_specs=[pl.BlockSpec((tq, A_VDIM), lambda h, i: (i, h)), kv_spec, kv_spec, vec, vec, vec],
        out_specs=pl.BlockSpec((tq, A_VDIM), lambda h, i: (i, h)),
        scratch_shapes=[pltpu.VMEM((2 * tq, A_VDIM), BF16), pltpu.VMEM((2 * tq, 2 * A_VDIM), F32)],
        compiler_params=_cparams(2),
        name="attn_prompt_shift",
    )(qn, kb, vb, shift, g, lam)


def _attn_prompt(qn, kb, vb, score_bound, g, lam, t, n_heads, out_rows):
    shift = jnp.full((1, LANES), score_bound, F32)
    return lax.cond(
        score_bound <= ATTN_SHIFT_MAX,
        lambda: _attn_prompt_shift(qn, kb, vb, shift, g, lam, t, n_heads, out_rows),
        lambda: _attn_prompt_running_max(qn, kb, vb, g, lam, t, n_heads, out_rows))


def _attn_sample_kernel(q_ref, k_ref, v_ref, g_ref, lam_ref, o_ref, *, tq, n_past, n_keys):
    q1, q2 = _stack_maps(q_ref[...])
    qs = jnp.concatenate([q1, q2], axis=0)
    s = lax.dot_general(qs, k_ref[0], NT_DIMS, preferred_element_type=F32)
    row = lax.broadcasted_iota(jnp.int32, s.shape, 0)
    col = lax.broadcasted_iota(jnp.int32, s.shape, 1)
    visible = (col // CHUNK <= (n_past + row % tq) // CHUNK) & (col < n_keys)
    s = jnp.where(visible, s, NEG_INF)
    p = jnp.exp2(s - jnp.max(s, axis=1, keepdims=True))
    p = jnp.where(col < n_keys, p, 0.0)
    l = jnp.sum(p, axis=1, keepdims=True)
    acc = jnp.dot(p.astype(BF16), v_ref[0], preferred_element_type=F32)
    o_ref[...] = _attn_finish(acc, l, tq, lam_ref[...], g_ref[...]).astype(o_ref.dtype)


def _into_buffer(kernel, n_inputs, out_buf):
    if out_buf is None:
        return kernel, [], (), {}

    def body(*refs):
        return kernel(*refs[:n_inputs], *refs[n_inputs + 1:])

    return body, [pl.BlockSpec(memory_space=pl.ANY)], (out_buf,), {n_inputs: 0}


def _attn_sample(qn, k_all, v_all, g, lam, out_buf, row0, n_batch, tq, n_heads, n_past, n_keys):
    kp = k_all.shape[1]
    q_blk0 = row0 // tq
    kv_spec = pl.BlockSpec((1, kp, A_VDIM), lambda b, h: (b, 0, h))
    vec = pl.BlockSpec((1, LANES), lambda b, h: (0, 0))
    row_spec = pl.BlockSpec((tq, A_VDIM), lambda b, h: (q_blk0 + b, h))
    body, buf_specs, buf_args, aliases = _into_buffer(
        functools.partial(_attn_sample_kernel, tq=tq, n_past=n_past, n_keys=n_keys), 5, out_buf)
    return pl.pallas_call(
        body,
        out_shape=jax.ShapeDtypeStruct(out_buf.shape, out_buf.dtype),
        grid=(n_batch, n_heads),
        in_specs=[row_spec, kv_spec, kv_spec, vec, vec] + buf_specs,
        out_specs=row_spec,
        input_output_aliases=aliases,
        compiler_params=_cparams(2),
        name="attn_sample",
    )(qn, k_all, v_all, g, lam, *buf_args)


def _hgrn_sum_matrix(L):
    t = np.arange(L)[:, None]
    s = np.arange(L)[None, :]
    mats = [s <= t]
    for lev in range(int(math.log2(L))):
        m = (t >> (lev + 1) << (lev + 1)) + (1 << lev) - 1
        mats.append(np.where(t > m, (s > m) & (s <= t), (s > t) & (s <= m)))
    return jnp.asarray(np.concatenate(mats, axis=0), BF16)


def _hgrn_kernel(q_ref, f_ref, i_ref, g_ref, m_ref, lb_ref, og_ref, s0_ref, o_ref, sfin_ref, s_sc, *, n_heads):
    c = pl.program_id(1)
    L = q_ref.shape[0]
    dk = B_HEAD_DIM

    @pl.when(c == 0)
    def _():
        s_sc[...] = s0_ref[0]

    qf = _silu(q_ref[...])
    lb = lb_ref[...]
    f = lb + (1.0 - lb) * _sigmoid(f_ref[...])
    kf = 1.0 - f
    vv = i_ref[...]
    gate = _silu(g_ref[...])

    lf_hi, lf_lo = _split(jnp.log(f))
    sums = (jnp.dot(m_ref[...], lf_hi, preferred_element_type=F32)
            + jnp.dot(m_ref[...], lf_lo, preferred_element_type=F32))
    b = sums[0:L]

    row = lax.broadcasted_iota(jnp.int32, (L, L), 0)
    col = lax.broadcasted_iota(jnp.int32, (L, L), 1)
    roww = lax.broadcasted_iota(jnp.int32, qf.shape, 0)
    atts = [jnp.zeros((L, L), F32) for _ in range(n_heads)]
    for lev in range(int(math.log2(L))):
        e = jnp.exp(sums[(lev + 1) * L:(lev + 2) * L])
        upper = ((roww >> lev) & 1) == 1
        qk = jnp.where(upper, qf * e, 0.0).astype(BF16)
        kk = jnp.where(upper, 0.0, kf * e).astype(BF16)
        same_block = (row >> (lev + 1)) == (col >> (lev + 1))
        for h in range(n_heads):
            sl = slice(h * dk, (h + 1) * dk)
            a = lax.dot_general(qk[:, sl], kk[:, sl], NT_DIMS, preferred_element_type=F32)
            atts[h] = atts[h] + jnp.where(same_block, a, 0.0)

    b_last = b[L - 1:L, :]
    qe = (qf * jnp.exp(b)).astype(BF16)
    kd = (kf * jnp.exp(b_last - b)).astype(BF16)
    vb = vv.astype(BF16)
    e_last = jnp.exp(b_last)
    eye = (lax.broadcasted_iota(jnp.int32, (dk, dk), 0) == lax.broadcasted_iota(jnp.int32, (dk, dk), 1))
    for h in range(n_heads):
        sl = slice(h * dk, (h + 1) * dk)
        att = atts[h] + jnp.where(row == col, jnp.sum(qf[:, sl] * kf[:, sl], axis=1, keepdims=True), 0.0)
        s_prev = s_sc[h]
        o = (jnp.dot(att.astype(BF16), vb[:, sl], preferred_element_type=F32)
             + jnp.dot(qe[:, sl], s_prev.astype(BF16), preferred_element_type=F32))
        decay = jnp.where(eye, jnp.broadcast_to(e_last[:, sl], (dk, dk)), 0.0)
        s_sc[h] = _dot3(decay, s_prev) + lax.dot_general(kd[:, sl], vb[:, sl], TN_DIMS, preferred_element_type=F32)
        o_ref[:, sl] = (_rms(o) * og_ref[...] * gate[:, sl]).astype(o_ref.dtype)
    sfin_ref[0] = s_sc[...]


def _hgrn(proj, lb, og, s0, out_buf, col0, row_blk0, n_batch, n_chunks, chunk, w_b):
    n_heads = w_b // B_HEAD_DIM
    cb = col0 // w_b
    sum_mat = _hgrn_sum_matrix(chunk)
    blk = lambda k: pl.BlockSpec((chunk, w_b), lambda b, c, k=k: (row_blk0 + b * n_chunks + c, cb + k))
    vec = lambda n: pl.BlockSpec((1, n), lambda b, c: (0, 0))
    st = pl.BlockSpec((1, n_heads, B_HEAD_DIM, B_HEAD_DIM), lambda b, c: (b, 0, 0, 0))
    body, buf_specs, buf_args, aliases = _into_buffer(functools.partial(_hgrn_kernel, n_heads=n_heads), 8, out_buf)
    return pl.pallas_call(
        body,
        out_shape=(jax.ShapeDtypeStruct((proj.shape[0], w_b), BF16), jax.ShapeDtypeStruct(s0.shape, F32)),
        grid=(n_batch, n_chunks),
        in_specs=[blk(0), blk(1), blk(2), blk(3), pl.BlockSpec(sum_mat.shape, lambda b, c: (0, 0)),
                  vec(w_b), vec(B_HEAD_DIM), st] + buf_specs,
        out_specs=(pl.BlockSpec((chunk, w_b), lambda b, c: (row_blk0 + b * n_chunks + c, 0)), st),
        scratch_shapes=[pltpu.VMEM((n_heads, B_HEAD_DIM, B_HEAD_DIM), F32)],
        input_output_aliases=aliases,
        compiler_params=_cparams(2),
        name="hgrn2",
    )(proj, proj, proj, proj, sum_mat, lb, og, s0, *buf_args)


def _gelu_tanh(x):
    return 0.5 * x * (1.0 + jnp.tanh(math.sqrt(2.0 / math.pi) * (x + 0.044715 * x * x * x)))


def _s5_kernel(u_ref, bre_ref, bim_ref, cre_ref, cim_ref, d_ref, tab_ref, x0re_ref, x0im_ref,
               glu_ref, og_ref, o_ref, fre_ref, fim_ref, xre_sc, xim_sc, cr_sc, ci_sc):
    t = pl.program_id(1)
    rows, w_c = u_ref.shape
    n_state = xre_sc.shape[1]
    n_blk = w_c // LANES
    sb = n_state // n_blk

    @pl.when(t == 0)
    def _():
        cr_sc[...] = jnp.broadcast_to(x0re_ref[0], cr_sc.shape)
        ci_sc[...] = jnp.broadcast_to(x0im_ref[0], ci_sc.shape)

    u = u_ref[...]
    for j in range(n_blk):
        uj = u[:, j * LANES:(j + 1) * LANES]
        xre_sc[:, j * sb:(j + 1) * sb] = _dot2(uj, bre_ref[j])
        xim_sc[:, j * sb:(j + 1) * sb] = _dot2(uj, bim_ref[j])

    half = n_state // 2

    def body(r, carry):
        r0 = pl.multiple_of(r * SUBLANES, SUBLANES)
        for hs in range(2):
            cs = slice(hs * half, (hs + 1) * half)
            xr = xre_sc[pl.ds(r0, SUBLANES), cs]
            xi = xim_sc[pl.ds(r0, SUBLANES), cs]
            for lev, shift in enumerate((1, 2, 4)):
                ar = tab_ref[2 * lev, :, cs]
                ai = tab_ref[2 * lev + 1, :, cs]
                sr = pltpu.roll(xr, shift, 0)
                si = pltpu.roll(xi, shift, 0)
                xr, xi = xr + (ar * sr - ai * si), xi + (ar * si + ai * sr)
            pr = tab_ref[6, :, cs]
            pi = tab_ref[7, :, cs]
            cr = cr_sc[:, cs]
            ci = ci_sc[:, cs]
            xr, xi = xr + (pr * cr - pi * ci), xi + (pr * ci + pi * cr)
            xre_sc[pl.ds(r0, SUBLANES), cs] = xr
            xim_sc[pl.ds(r0, SUBLANES), cs] = xi
            cr_sc[:, cs] = jnp.broadcast_to(xr[SUBLANES - 1:SUBLANES, :], (SUBLANES, half))
            ci_sc[:, cs] = jnp.broadcast_to(xi[SUBLANES - 1:SUBLANES, :], (SUBLANES, half))
        return carry

    lax.fori_loop(0, rows // SUBLANES, body, 0)

    ys = []
    for j in range(n_blk):
        ys.append(_dot2(xre_sc[:, j * sb:(j + 1) * sb], cre_ref[j])
                  - _dot2(xim_sc[:, j * sb:(j + 1) * sb], cim_ref[j]))
    yc = _gelu_tanh(jnp.concatenate(ys, axis=1) + d_ref[...] * u)
    gl = jnp.dot(yc.astype(BF16), glu_ref[...], preferred_element_type=F32)
    v = gl[:, :w_c] * _sigmoid(gl[:, w_c:])
    o_ref[...] = (_rms(v) * og_ref[...]).astype(o_ref.dtype)
    fre_ref[0] = cr_sc[0:1, :]
    fim_ref[0] = ci_sc[0:1, :]


def _s5(proj, wts, x0re, x0im, out_buf, col0, row_blk0, n_batch, n_tiles, tile, w_c):
    bre, bim, cre, cim, dvec, tab, glu, og = wts
    n_state = tab.shape[2]
    cb = col0 // w_c
    full = lambda a: pl.BlockSpec(a.shape, lambda b, t, nd=a.ndim: (0,) * nd)
    st = pl.BlockSpec((1, 1, n_state), lambda b, t: (b, 0, 0))
    body, buf_specs, buf_args, aliases = _into_buffer(_s5_kernel, 11, out_buf)
    return pl.pallas_call(
        body,
        out_shape=(jax.ShapeDtypeStruct((proj.shape[0], w_c), BF16),
                   jax.ShapeDtypeStruct((n_batch, 1, n_state), F32),
                   jax.ShapeDtypeStruct((n_batch, 1, n_state), F32)),
        grid=(n_batch, n_tiles),
        in_specs=[pl.BlockSpec((tile, w_c), lambda b, t: (row_blk0 + b * n_tiles + t, cb)),
                  full(bre), full(bim), full(cre), full(cim), full(dvec), full(tab), st, st,
                  full(glu), full(og)] + buf_specs,
        out_specs=(pl.BlockSpec((tile, w_c), lambda b, t: (row_blk0 + b * n_tiles + t, 0)), st, st),
        scratch_shapes=[pltpu.VMEM((tile, n_state), F32), pltpu.VMEM((tile, n_state), F32),
                        pltpu.VMEM((SUBLANES, n_state), F32), pltpu.VMEM((SUBLANES, n_state), F32)],
        input_output_aliases=aliases,
        compiler_params=_cparams(2),
        name="s5",
    )(proj, bre, bim, cre, cim, dvec, tab, x0re, x0im, glu, og, *buf_args)


def _s5_weights(a_re, a_im, b_re, b_im, c_re, c_im, d, log_dt, glu_w, out_g):
    g, n = a_re.shape
    p = b_re.shape[2]
    lam_re = jnp.minimum(a_re.astype(F32), A_RE_MAX)
    lam_im = a_im.astype(F32)
    dt = jnp.exp(log_dt.astype(F32))[:, None]
    mag = jnp.exp(lam_re * dt)
    abar_re = mag * jnp.cos(lam_im * dt)
    abar_im = mag * jnp.sin(lam_im * dt)
    den = lam_re * lam_re + lam_im * lam_im
    z_re = abar_re - 1.0
    coef_re = (z_re * lam_re + abar_im * lam_im) / den
    coef_im = (abar_im * lam_re - z_re * lam_im) / den
    b_re = b_re.astype(F32)
    b_im = b_im.astype(F32)
    bbar_re = coef_re[..., None] * b_re - coef_im[..., None] * b_im
    bbar_im = coef_re[..., None] * b_im + coef_im[..., None] * b_re

    gpb = LANES // p
    n_blk = g // gpb
    eye = jnp.eye(gpb, dtype=F32)

    def blockdiag_in(bb):
        x = bb.reshape(n_blk, gpb, n, p)
        return jnp.einsum('jgnp,gh->jgphn', x, eye).reshape(n_blk, gpb * p, gpb * n)

    def blockdiag_out(cc):
        x = cc.astype(F32).reshape(n_blk, gpb, p, n)
        return jnp.einsum('jgpn,gh->jgnhp', x, eye).reshape(n_blk, gpb * n, gpb * p)

    pw_re, pw_im = [abar_re.reshape(-1)], [abar_im.reshape(-1)]
    for _ in range(SUBLANES - 1):
        pr, pi = pw_re[-1], pw_im[-1]
        pw_re.append(pr * pw_re[0] - pi * pw_im[0])
        pw_im.append(pr * pw_im[0] + pi * pw_re[0])
    rows = jnp.arange(SUBLANES)[:, None]
    tabs = []
    for k in (1, 2, 4):
        keep = rows >= k
        tabs.append(jnp.where(keep, pw_re[k - 1][None, :], 0.0))
        tabs.append(jnp.where(keep, pw_im[k - 1][None, :], 0.0))
    tabs.append(jnp.stack(pw_re))
    tabs.append(jnp.stack(pw_im))
    tab = jnp.stack(tabs)
    return (blockdiag_in(bbar_re).astype(BF16), blockdiag_in(bbar_im).astype(BF16),
            blockdiag_out(c_re).astype(BF16), blockdiag_out(c_im).astype(BF16),
            d.astype(F32).reshape(1, g * p), tab, glu_w.astype(BF16), out_g.astype(F32).reshape(1, -1))


def _take_rows(a, idx):
    return a.at[idx].get(mode="promise_in_bounds")


def _moe_plan(route, n_experts, m_pad):
    n_tok = route.shape[0]
    n = TOP_K * n_tok
    tm = MOE_ROW_TILE
    e_flat = jnp.concatenate([route[:, 0], route[:, 1]]).astype(jnp.int32)
    onehot = (e_flat[:, None] == jnp.arange(n_experts, dtype=jnp.int32)[None, :]).astype(jnp.int32)
    csum = jnp.cumsum(onehot, axis=0)
    counts = csum[-1]
    rank = jnp.sum(csum * onehot, axis=1) - 1
    padded = (counts + tm - 1) // tm * tm
    ends_pad = jnp.cumsum(padded)
    start_pad = ends_pad - padded
    start = jnp.cumsum(counts) - counts
    pos = jnp.sum(onehot * start_pad[None, :], axis=1) + rank
    order = jnp.argsort(e_flat, stable=True)
    n_tiles = m_pad // tm
    tile_start = jnp.arange(n_tiles, dtype=jnp.int32) * tm
    tile_expert = jnp.sum((tile_start[:, None] >= ends_pad[None, :]).astype(jnp.int32), axis=1)
    tile_expert = jnp.minimum(tile_expert, jnp.max(e_flat))
    k = jnp.arange(m_pad, dtype=jnp.int32) - jnp.repeat(start_pad[tile_expert], tm)
    valid = k < jnp.repeat(counts[tile_expert], tm)
    slot = jnp.clip(jnp.repeat(start[tile_expert], tm) + k, 0, n - 1)
    row_token = jnp.where(valid, _take_rows(order, slot) % n_tok, 0)
    n_valid = (ends_pad[-1] // tm).astype(jnp.int32)
    return row_token, pos[:n_tok], pos[n_tok:], tile_expert, n_valid.reshape(1)


def _cast_pad_kernel(w_ref, o_ref):
    n = w_ref.shape[1]
    o_ref[:, :n] = w_ref[...].astype(o_ref.dtype)
    if o_ref.shape[1] > n:
        o_ref[:, n:] = jnp.zeros((o_ref.shape[0], o_ref.shape[1] - n), o_ref.dtype)


def _cast_pad_cols(w, n_pad):
    lead, n = w.shape[:-1], w.shape[-1]
    rows = math.prod(lead)
    tr = max(r for r in (1024, 512, 256, 128, 64, 32, 16)
             if rows % r == 0 and (r * n * 4 <= CAST_BLOCK_BYTES or r == 16))
    out = pl.pallas_call(
        _cast_pad_kernel,
        out_shape=jax.ShapeDtypeStruct((rows, n_pad), BF16),
        grid=(rows // tr,),
        in_specs=[pl.BlockSpec((tr, n), lambda i: (i, 0))],
        out_specs=pl.BlockSpec((tr, n_pad), lambda i: (i, 0)),
        compiler_params=_cparams(1),
        name="cast_pad",
    )(w.reshape(rows, n))
    return out.reshape(lead + (n_pad,))


def kernel(x_prompt, x_sample, c_prompt, c_sample, cache_k, cache_v, state_hgrn, state_ssm_re, state_ssm_im, w_ada, b_ada, norm1_g, norm2_g, w_in, w_out, q_norm_g, k_norm_g, lambda_qk, attn_out_g, hgrn_lower_bounds, hgrn_out_g, ssm_a_re, ssm_a_im, ssm_b_re, ssm_b_im, ssm_c_re, ssm_c_im, ssm_d, ssm_log_dt, ssm_glu_w, ssm_out_g, ffn_w1, ffn_w3, ffn_w2, moe_router, moe_w1, moe_w3, moe_w2):
    bp, t, d = x_prompt.shape
    bs, ts, _ = x_sample.shape
    depth = w_in.shape[0]
    n_past = cache_k.shape[2]
    n_heads = cache_k.shape[3]
    w_a = n_heads * A_VDIM
    w_b = hgrn_lower_bounds.shape[1]
    w_c = ssm_out_g.shape[1]
    n_state = ssm_a_re.shape[1] * ssm_a_re.shape[2]
    n_sample = bs * ts
    assert bp == 1 and t % ROW_TILE == 0 and t % ATTN_TILE == 0 and t % ATTN_FAST_TILE == 0
    assert t % S5_TILE == 0 and t % HGRN_TILE == 0
    assert n_sample % ROW_TILE == 0 and ROW_TILE % ts == 0 and ts <= CHUNK and ts % 16 == 0
    assert (ts & (ts - 1)) == 0 and w_a == 2 * w_b and w_b == w_c
    n_pt = t // ROW_TILE
    m_tot = t + n_sample

    x = jnp.concatenate([x_prompt.reshape(t, d), x_sample.reshape(n_sample, d)], axis=0)

    c_rows = jnp.concatenate([c_prompt, c_sample], axis=0)
    c_rows = jnp.pad(c_rows, ((0, _round_up(1 + bs, 16) - (1 + bs)), (0, 0)))
    mod = _modulation(c_rows.astype(F32), w_ada, b_ada)

    def mod_blocks(l, k):
        m = mod[l, :, k * d:(k + 1) * d]
        return jnp.concatenate([jnp.broadcast_to(m[0:1], (ROW_TILE, d)),
                                jnp.repeat(m[1:1 + bs], ts, axis=0)], axis=0)

    lb_p = jax.nn.softmax(hgrn_lower_bounds.astype(F32), axis=0)
    lower_bounds = jnp.cumsum(lb_p, axis=0) - lb_p[0]

    zero_hgrn = jnp.zeros((bp,) + state_hgrn.shape[2:], F32)
    zero_ssm = jnp.zeros((bp, 1, n_state), F32)
    n_keys = n_past + ts
    keys_pad = _round_up(n_keys, LANES)

    kp, vp, sp, rep, imp = [], [], [], [], []
    ks, vs, ss, res, ims = [], [], [], [], []
    for l in range(depth):
        sh1, sc1, g1, sh2, sc2, g2 = [mod_blocks(l, k) for k in range(6)]
        h = _norm_mod(x, norm1_g[l].reshape(1, d), sc1, sh1, n_pt)
        proj = _matmul(h, _cast_pad_cols(w_in[l], w_in.shape[2]))

        lam_init = 0.8 - 0.6 * math.exp(-0.3 * l)
        lq = lambda_qk[l].astype(F32)
        lam = jnp.exp(jnp.sum(lq[0] * lq[1])) - jnp.exp(jnp.sum(lq[2] * lq[3])) + lam_init
        lam_vec = jnp.full((1, LANES), lam, F32)
        og_a = (attn_out_g[l].astype(F32) * (1.0 - lam_init)).reshape(1, A_VDIM)
        gq = jnp.tile(q_norm_g[l].astype(F32), 2).reshape(1, LANES)
        gk = jnp.tile(k_norm_g[l].astype(F32), 2).reshape(1, LANES)
        qn, kn, kb, vb = _qknorm(proj, gq, gk, w_a)
        score_bound = 1.02 * A_HEAD_DIM * Q_SCALE * jnp.max(jnp.abs(gq)) * jnp.max(jnp.abs(gk))
        oa = _attn_prompt(qn, kb, vb, score_bound, og_a, lam_vec, t, n_heads, m_tot)
        pad = jnp.zeros((bs, keys_pad - n_keys, w_a), BF16)
        k_all = jnp.concatenate([cache_k[l].reshape(bs, n_past, w_a).astype(BF16),
                                 kb[t:].reshape(bs, ts, w_a), pad], axis=1)
        v_all = jnp.concatenate([cache_v[l].reshape(bs, n_past, w_a).astype(BF16),
                                 vb[t:].reshape(bs, ts, w_a), pad], axis=1)
        oa = _attn_sample(qn, k_all, v_all, og_a, lam_vec, oa, t, bs, ts, n_heads, n_past, n_keys)

        lb = lower_bounds[l].reshape(1, w_b)
        og_b = hgrn_out_g[l].astype(F32).reshape(1, B_HEAD_DIM)
        col_b = 3 * w_a
        ob, s_p = _hgrn(proj, lb, og_b, zero_hgrn, None, col_b, 0, bp, t // HGRN_TILE, HGRN_TILE, w_b)
        ob, s_s = _hgrn(proj, lb, og_b, state_hgrn[l].astype(F32), ob, col_b, t // ts, bs, 1, ts, w_b)

        wts = _s5_weights(ssm_a_re[l], ssm_a_im[l], ssm_b_re[l], ssm_b_im[l], ssm_c_re[l], ssm_c_im[l],
                          ssm_d[l], ssm_log_dt[l], ssm_glu_w[l], ssm_out_g[l])
        col_c = 3 * w_a + 4 * w_b
        oc, re_p, im_p = _s5(proj, wts, zero_ssm, zero_ssm, None, col_c, 0, bp, t // S5_TILE, S5_TILE, w_c)
        oc, re_s, im_s = _s5(proj, wts, state_ssm_re[l].astype(F32).reshape(bs, 1, n_state),
                             state_ssm_im[l].astype(F32).reshape(bs, 1, n_state),
                             oc, col_c, t // ts, bs, 1, ts, w_c)

        x = _matmul_residual((oa, ob, oc), _cast_pad_cols(w_out[l], d), x, g1, n_pt, tn_target=1024)

        if l % 2 == 0:
            j = l // 2
            dff = ffn_w1.shape[2]
            dff_pad = _round_up(dff, 512)
            h2 = _norm_mod(x, norm2_g[l].reshape(1, d), sc2, sh2, n_pt)
            a = _swiglu_up(h2, _cast_pad_cols(ffn_w1[j], dff_pad), _cast_pad_cols(ffn_w3[j], dff_pad))
            x = _matmul_residual(a, _cast_pad_cols(ffn_w2[j], d), x, g2, n_pt)
        else:
            j = l // 2
            n_experts = moe_router.shape[2]
            dff = moe_w1.shape[3]
            dff_pad = _round_up(dff, 512)
            h2, route = _norm_mod(x, norm2_g[l].reshape(1, d), sc2, sh2, n_pt, w_router=moe_router[j])
            m_pad = TOP_K * m_tot + n_experts * MOE_ROW_TILE
            row_token, pos0, pos1, tile_expert, n_valid = _moe_plan(route, n_experts, m_pad)
            up = _grouped_swiglu_up(_take_rows(h2, row_token), _cast_pad_cols(moe_w1[j], dff_pad),
                                    _cast_pad_cols(moe_w3[j], dff_pad), tile_expert, n_valid)
            down = _grouped_down(up, _cast_pad_cols(moe_w2[j], d), tile_expert, n_valid, tn_target=1024)
            x = _moe_combine(x, _take_rows(down, pos0), _take_rows(down, pos1), route, g2, n_pt)

        v_all_tokens = proj[:, 2 * w_a:3 * w_a]
        kp.append(kn[:t].reshape(bp, t, n_heads, 2, A_HEAD_DIM))
        vp.append(v_all_tokens[:t].reshape(bp, t, n_heads, A_VDIM))
        sp.append(s_p)
        rep.append(re_p.reshape(bp, -1, C_STATE))
        imp.append(im_p.reshape(bp, -1, C_STATE))
        ks.append(kn[t:].reshape(bs, ts, n_heads, 2, A_HEAD_DIM))
        vs.append(v_all_tokens[t:].reshape(bs, ts, n_heads, A_VDIM))
        ss.append(s_s)
        res.append(re_s.reshape(bs, -1, C_STATE))
        ims.append(im_s.reshape(bs, -1, C_STATE))

    return (x[:t].reshape(bp, t, d), x[t:].reshape(bs, ts, d),
            jnp.stack(kp), jnp.stack(vp), jnp.stack(sp), jnp.stack(rep), jnp.stack(imp),
            jnp.stack(ks), jnp.stack(vs), jnp.stack(ss), jnp.stack(res), jnp.stack(ims))
```

```python
import functools
import math

import numpy as np
import jax
import jax.numpy as jnp
from jax import lax
from jax.experimental import pallas as pl
from jax.experimental.pallas import tpu as pltpu

F32 = jnp.float32
BF16 = jnp.bfloat16

CHUNK = 64
A_HEAD_DIM = 64
A_VDIM = 2 * A_HEAD_DIM
B_HEAD_DIM = 128
C_GROUP = 16
C_STATE = 64
TOP_K = 2
RMS_EPS = 1e-6
NEG_INF = -1e30
A_RE_MAX = -1e-4

LANES = 128
SUBLANES = 8
ROW_TILE = 512
MOE_ROW_TILE = 512
ATTN_TILE = 512
ATTN_FAST_TILE = 1024
ATTN_DIAG_SUB = 256
ATTN_SHIFT_MAX = 60.0
HGRN_TILE = 256
COMBINE_COLS = 1024
S5_TILE = 512
VMEM_LIMIT = 48 * 1024 * 1024
CAST_BLOCK_BYTES = 6 * 1024 * 1024

Q_SCALE = (A_HEAD_DIM ** -0.5) * math.log2(math.e)

NT_DIMS = (((1,), (1,)), ((), ()))
TN_DIMS = (((0,), (0,)), ((), ()))
NN_DIMS = (((1,), (0,)), ((), ()))


def _cparams(n_axes):
    return pltpu.CompilerParams(
        dimension_semantics=("arbitrary",) * n_axes, vmem_limit_bytes=VMEM_LIMIT)


def _round_up(x, m):
    return (x + m - 1) // m * m


def _pick_tile(n, target):
    best = LANES
    for t in range(LANES, min(n, target) + 1, LANES):
        if n % t == 0:
            best = t
    return best


def _split(x):
    hi = x.astype(BF16)
    lo = (x - hi.astype(F32)).astype(BF16)
    return hi, lo


def _dot3(a, b, dims=NN_DIMS):
    ah, al = _split(a)
    bh, bl = _split(b)
    d = functools.partial(lax.dot_general, dimension_numbers=dims, preferred_element_type=F32)
    return d(ah, bh) + d(al, bh) + d(ah, bl)


def _dot2(a, w):
    ah, al = _split(a)
    return jnp.dot(ah, w, preferred_element_type=F32) + jnp.dot(al, w, preferred_element_type=F32)


def _sigmoid(x):
    return 1.0 / (1.0 + jnp.exp(-x))


def _silu(x):
    return x * _sigmoid(x)


def _rms(x):
    return x * lax.rsqrt(jnp.mean(x * x, axis=-1, keepdims=True) + RMS_EPS)


def _mod_kernel(c_ref, w_ref, b_ref, o_ref):
    o_ref[0] = _dot3(_silu(c_ref[...]), w_ref[0]) + b_ref[0]


def _modulation(c_rows, w_ada, b_ada):
    depth, d, n = w_ada.shape
    rows = c_rows.shape[0]
    tn = _pick_tile(n, 1024)
    return pl.pallas_call(
        _mod_kernel,
        out_shape=jax.ShapeDtypeStruct((depth, rows, n), F32),
        grid=(depth, n // tn),
        in_specs=[
            pl.BlockSpec((rows, d), lambda l, j: (0, 0)),
            pl.BlockSpec((1, d, tn), lambda l, j: (l, 0, j)),
            pl.BlockSpec((1, 1, tn), lambda l, j: (l, 0, j)),
        ],
        out_specs=pl.BlockSpec((1, rows, tn), lambda l, j: (l, 0, j)),
        compiler_params=_cparams(2),
        name="adaln_mod",
    )(c_rows, w_ada, b_ada.reshape(depth, 1, n))


def _norm_mod_kernel(x_ref, g_ref, sc_ref, sh_ref, h_ref):
    y = _rms(x_ref[...]) * g_ref[...]
    h_ref[...] = (y * (1.0 + sc_ref[...]) + sh_ref[...]).astype(h_ref.dtype)


def _norm_mod_route_kernel(x_ref, g_ref, sc_ref, sh_ref, wr_ref, h_ref, r_ref, *, n_experts):
    y = _rms(x_ref[...]) * g_ref[...]
    h = y * (1.0 + sc_ref[...]) + sh_ref[...]
    h_ref[...] = h.astype(h_ref.dtype)
    logits = _dot3(h, wr_ref[...])
    lane = lax.broadcasted_iota(jnp.int32, logits.shape, 1).astype(F32)
    lg = jnp.where(lane < n_experts, logits, -jnp.inf)
    m1 = jnp.max(lg, axis=1, keepdims=True)
    i1 = jnp.min(jnp.where(lg == m1, lane, float(LANES)), axis=1, keepdims=True)
    lg2 = jnp.where(lane == i1, -jnp.inf, lg)
    m2 = jnp.max(lg2, axis=1, keepdims=True)
    i2 = jnp.min(jnp.where(lg2 == m2, lane, float(LANES)), axis=1, keepdims=True)
    e = jnp.exp(m2 - m1)
    inv = 1.0 / (1.0 + e)
    r_ref[...] = jnp.where(lane == 0.0, i1, jnp.where(lane == 1.0, i2, jnp.where(
        lane == 2.0, inv, jnp.where(lane == 3.0, e * inv, 0.0))))


def _mod_index(n_prompt_tiles):
    return lambda i: (jnp.maximum(i - n_prompt_tiles + 1, 0), 0)


def _norm_mod(x, g, sc, sh, n_prompt_tiles, w_router=None):
    m, d = x.shape
    mi = _mod_index(n_prompt_tiles)
    in_specs = [
        pl.BlockSpec((ROW_TILE, d), lambda i: (i, 0)),
        pl.BlockSpec((1, d), lambda i: (0, 0)),
        pl.BlockSpec((ROW_TILE, d), mi),
        pl.BlockSpec((ROW_TILE, d), mi),
    ]
    h_spec = pl.BlockSpec((ROW_TILE, d), lambda i: (i, 0))
    if w_router is None:
        return pl.pallas_call(
            _norm_mod_kernel,
            out_shape=jax.ShapeDtypeStruct((m, d), BF16),
            grid=(m // ROW_TILE,),
            in_specs=in_specs,
            out_specs=h_spec,
            compiler_params=_cparams(1),
            name="norm_mod",
        )(x, g, sc, sh)
    n_experts = w_router.shape[1]
    wr = jnp.zeros((d, LANES), F32).at[:, :n_experts].set(w_router.astype(F32))
    return pl.pallas_call(
        functools.partial(_norm_mod_route_kernel, n_experts=n_experts),
        out_shape=(jax.ShapeDtypeStruct((m, d), BF16), jax.ShapeDtypeStruct((m, LANES), F32)),
        grid=(m // ROW_TILE,),
        in_specs=in_specs + [pl.BlockSpec((d, LANES), lambda i: (0, 0))],
        out_specs=(h_spec, pl.BlockSpec((ROW_TILE, LANES), lambda i: (i, 0))),
        compiler_params=_cparams(1),
        name="norm_mod_route",
    )(x, g, sc, sh, wr)


def _mm_kernel(a_ref, w_ref, o_ref):
    o_ref[...] = jnp.dot(a_ref[...], w_ref[...], preferred_element_type=F32).astype(o_ref.dtype)


def _mm_res_kernel(*refs, widths):
    a_refs, (w_ref, x_ref, g_ref, o_ref) = refs[:len(widths)], refs[len(widths):]
    acc, k0 = None, 0
    for a_ref, kw in zip(a_refs, widths):
        part = jnp.dot(a_ref[...], w_ref[k0:k0 + kw, :], preferred_element_type=F32)
        acc = part if acc is None else acc + part
        k0 += kw
    o_ref[...] = x_ref[...] + g_ref[...] * acc


def _swiglu_up_kernel(a_ref, w1_ref, w3_ref, o_ref):
    a = a_ref[...]
    u = jnp.dot(a, w1_ref[...], preferred_element_type=F32)
    v = jnp.dot(a, w3_ref[...], preferred_element_type=F32)
    o_ref[...] = (_silu(u) * v).astype(o_ref.dtype)


def _matmul(a, w, tn_target=1408):
    m, k = a.shape
    n = w.shape[1]
    tn = _pick_tile(n, tn_target)
    return pl.pallas_call(
        _mm_kernel,
        out_shape=jax.ShapeDtypeStruct((m, n), F32),
        grid=(n // tn, m // ROW_TILE),
        in_specs=[
            pl.BlockSpec((ROW_TILE, k), lambda j, i: (i, 0)),
            pl.BlockSpec((k, tn), lambda j, i: (0, j)),
        ],
        out_specs=pl.BlockSpec((ROW_TILE, tn), lambda j, i: (i, j)),
        compiler_params=_cparams(2),
        name="matmul",
    )(a, w)


def _matmul_residual(a, w, x, gate, n_prompt_tiles, tn_target=512):
    k, n = w.shape
    parts = a if isinstance(a, tuple) else (a,)
    widths = tuple(p.shape[1] for p in parts) if isinstance(a, tuple) else (k,)
    assert sum(widths) == k
    m = parts[0].shape[0]
    tn = _pick_tile(n, tn_target)
    return pl.pallas_call(
        functools.partial(_mm_res_kernel, widths=widths),
        out_shape=jax.ShapeDtypeStruct((m, n), F32),
        grid=(n // tn, m // ROW_TILE),
        in_specs=[pl.BlockSpec((ROW_TILE, kw), lambda j, i: (i, 0)) for kw in widths] + [
            pl.BlockSpec((k, tn), lambda j, i: (0, j)),
            pl.BlockSpec((ROW_TILE, tn), lambda j, i: (i, j)),
            pl.BlockSpec((ROW_TILE, tn), lambda j, i: (jnp.maximum(i - n_prompt_tiles + 1, 0), j)),
        ],
        out_specs=pl.BlockSpec((ROW_TILE, tn), lambda j, i: (i, j)),
        compiler_params=_cparams(2),
        name="matmul_residual",
    )(*parts, w, x, gate)


def _swiglu_up(a, w1, w3, tn_target=1408):
    m, k = a.shape
    n = w1.shape[1]
    tn = _pick_tile(n, tn_target)
    return pl.pallas_call(
        _swiglu_up_kernel,
        out_shape=jax.ShapeDtypeStruct((m, n), BF16),
        grid=(n // tn, m // ROW_TILE),
        in_specs=[
            pl.BlockSpec((ROW_TILE, k), lambda j, i: (i, 0)),
            pl.BlockSpec((k, tn), lambda j, i: (0, j)),
            pl.BlockSpec((k, tn), lambda j, i: (0, j)),
        ],
        out_specs=pl.BlockSpec((ROW_TILE, tn), lambda j, i: (i, j)),
        compiler_params=_cparams(2),
        name="swiglu_up",
    )(a, w1, w3)


def _g_up_kernel(te_ref, nv_ref, a_ref, w1_ref, w3_ref, o_ref):
    i = pl.program_id(1)

    @pl.when(i < nv_ref[0])
    def _():
        a = a_ref[...]
        u = jnp.dot(a, w1_ref[0], preferred_element_type=F32)
        v = jnp.dot(a, w3_ref[0], preferred_element_type=F32)
        o_ref[...] = (_silu(u) * v).astype(o_ref.dtype)

    @pl.when(i >= nv_ref[0])
    def _():
        o_ref[...] = jnp.zeros_like(o_ref)


def _g_down_kernel(te_ref, nv_ref, a_ref, w_ref, o_ref, wb_sc):
    i = pl.program_id(1)

    @pl.when((i == 0) | (te_ref[i] != te_ref[jnp.maximum(i - 1, 0)]))
    def _():
        wb_sc[...] = w_ref[0].astype(BF16)

    @pl.when(i < nv_ref[0])
    def _():
        o_ref[...] = jnp.dot(a_ref[...], wb_sc[...], preferred_element_type=F32)

    @pl.when(i >= nv_ref[0])
    def _():
        o_ref[...] = jnp.zeros_like(o_ref)


def _grouped_swiglu_up(a, w1, w3, tile_expert, n_valid, tn_target=1408):
    m, k = a.shape
    n = w1.shape[2]
    tn = _pick_tile(n, tn_target)
    tm = MOE_ROW_TILE
    grid_spec = pltpu.PrefetchScalarGridSpec(
        num_scalar_prefetch=2,
        grid=(n // tn, m // tm),
        in_specs=[
            pl.BlockSpec((tm, k), lambda j, i, te, nv: (i, 0)),
            pl.BlockSpec((1, k, tn), lambda j, i, te, nv: (te[i], 0, j)),
            pl.BlockSpec((1, k, tn), lambda j, i, te, nv: (te[i], 0, j)),
        ],
        out_specs=pl.BlockSpec((tm, tn), lambda j, i, te, nv: (i, j)),
    )
    return pl.pallas_call(
        _g_up_kernel,
        out_shape=jax.ShapeDtypeStruct((m, n), BF16),
        grid_spec=grid_spec,
        compiler_params=_cparams(2),
        name="moe_swiglu_up",
    )(tile_expert, n_valid, a, w1, w3)


def _grouped_down(a, w, tile_expert, n_valid, tn_target=512):
    m = a.shape[0]
    _, k, n = w.shape
    tn = _pick_tile(n, tn_target)
    tm = MOE_ROW_TILE
    grid_spec = pltpu.PrefetchScalarGridSpec(
        num_scalar_prefetch=2,
        grid=(n // tn, m // tm),
        in_specs=[
            pl.BlockSpec((tm, k), lambda j, i, te, nv: (i, 0)),
            pl.BlockSpec((1, k, tn), lambda j, i, te, nv: (te[i], 0, j)),
        ],
        out_specs=pl.BlockSpec((tm, tn), lambda j, i, te, nv: (i, j)),
        scratch_shapes=[pltpu.VMEM((k, tn), BF16)],
    )
    return pl.pallas_call(
        _g_down_kernel,
        out_shape=jax.ShapeDtypeStruct((m, n), F32),
        grid_spec=grid_spec,
        compiler_params=_cparams(2),
        name="moe_down",
    )(tile_expert, n_valid, a, w.astype(F32))


def _moe_combine_kernel(x_ref, r0_ref, r1_ref, rt_ref, g_ref, o_ref):
    rt = rt_ref[...]
    y = rt[:, 2:3] * r0_ref[...] + rt[:, 3:4] * r1_ref[...]
    o_ref[...] = x_ref[...] + g_ref[...] * y


def _moe_combine_split_kernel(x_ref, r0_ref, r1_ref, rt_ref, g_ref, op_ref, os_ref, *, n_prompt_tiles):
    i = pl.program_id(1)
    rt = rt_ref[...]
    y = x_ref[...] + g_ref[...] * (rt[:, 2:3] * r0_ref[...] + rt[:, 3:4] * r1_ref[...])

    @pl.when(i < n_prompt_tiles)
    def _():
        op_ref[...] = y

    @pl.when(i >= n_prompt_tiles)
    def _():
        os_ref[...] = y


def _moe_combine(x, r0, r1, route, gate, n_prompt_tiles, split=False):
    m, d = x.shape
    tc = _pick_tile(d, COMBINE_COLS)
    row = pl.BlockSpec((ROW_TILE, tc), lambda c, i: (i, c))
    in_specs = [row, row, row, pl.BlockSpec((ROW_TILE, LANES), lambda c, i: (i, 0)),
                pl.BlockSpec((ROW_TILE, tc), lambda c, i: (jnp.maximum(i - n_prompt_tiles + 1, 0), c))]
    grid = (d // tc, m // ROW_TILE)
    if not split:
        return pl.pallas_call(
            _moe_combine_kernel,
            out_shape=jax.ShapeDtypeStruct((m, d), F32),
            grid=grid,
            in_specs=in_specs,
            out_specs=row,
            compiler_params=_cparams(2),
            name="moe_combine",
        )(x, r0, r1, route, gate)
    t = n_prompt_tiles * ROW_TILE
    return pl.pallas_call(
        functools.partial(_moe_combine_split_kernel, n_prompt_tiles=n_prompt_tiles),
        out_shape=(jax.ShapeDtypeStruct((t, d), F32), jax.ShapeDtypeStruct((m - t, d), F32)),
        grid=grid,
        in_specs=in_specs,
        out_specs=(pl.BlockSpec((ROW_TILE, tc), lambda c, i: (jnp.minimum(i, n_prompt_tiles - 1), c)),
                   pl.BlockSpec((ROW_TILE, tc), lambda c, i: (jnp.maximum(i - n_prompt_tiles, 0), c))),
        compiler_params=_cparams(2),
        name="moe_combine_split",
    )(x, r0, r1, route, gate)


def _qknorm_kernel(q_ref, k_ref, v_ref, gq_ref, gk_ref, qn_ref, kb_ref, vb_ref, kp_ref, ks_ref, vp_ref, vs_ref,
                   kn_sc, *, n_prompt_tiles):
    i = pl.program_id(0)
    r = lax.broadcasted_iota(jnp.int32, (LANES, LANES), 0) // A_HEAD_DIM
    c = lax.broadcasted_iota(jnp.int32, (LANES, LANES), 1) // A_HEAD_DIM
    seg = jnp.where(r == c, 1.0 / A_HEAD_DIM, 0.0).astype(BF16)

    def norm(x, g):
        hi, lo = _split(x * x)
        ms = (jnp.dot(hi, seg, preferred_element_type=F32)
              + jnp.dot(lo, seg, preferred_element_type=F32))
        return x * lax.rsqrt(ms + RMS_EPS) * g

    for h in range(q_ref.shape[1] // LANES):
        sl = slice(h * LANES, (h + 1) * LANES)
        qn = norm(q_ref[:, sl], gq_ref[...])
        qn_ref[:, sl] = (qn * Q_SCALE).astype(BF16)
        kn = norm(k_ref[:, sl], gk_ref[...])
        kn_sc[:, sl] = kn
        kb_ref[:, sl] = kn.astype(BF16)
    vb_ref[...] = v_ref[...].astype(BF16)

    @pl.when(i < n_prompt_tiles)
    def _():
        kp_ref[0] = kn_sc[...]
        vp_ref[0] = v_ref[...]

    @pl.when(i >= n_prompt_tiles)
    def _():
        ks_ref[0] = kn_sc[...]
        vs_ref[0] = v_ref[...]


def _qknorm(proj, gq, gk, w_a, layer, depth, n_prompt_tiles, kv_bufs):
    m = proj.shape[0]
    t = n_prompt_tiles * ROW_TILE
    blk = lambda c: pl.BlockSpec((ROW_TILE, w_a), lambda i, c=c: (i, c))
    g_spec = pl.BlockSpec((1, LANES), lambda i: (0, 0))
    out = pl.BlockSpec((ROW_TILE, w_a), lambda i: (i, 0))
    p_spec = pl.BlockSpec((1, ROW_TILE, w_a), lambda i: (layer, jnp.minimum(i, n_prompt_tiles - 1), 0))
    s_spec = pl.BlockSpec((1, ROW_TILE, w_a), lambda i: (layer, jnp.maximum(i - n_prompt_tiles, 0), 0))
    body, buf_specs, buf_args, aliases = _into_buffers(
        functools.partial(_qknorm_kernel, n_prompt_tiles=n_prompt_tiles), 5, kv_bufs, (3, 4, 5, 6))
    bf = jax.ShapeDtypeStruct((m, w_a), BF16)
    kv_p = jax.ShapeDtypeStruct((depth, t, w_a), F32)
    kv_s = jax.ShapeDtypeStruct((depth, m - t, w_a), F32)
    qn, kb, vb, k_p, k_s, v_p, v_s = pl.pallas_call(
        body,
        out_shape=(bf, bf, bf, kv_p, kv_s, kv_p, kv_s),
        grid=(m // ROW_TILE,),
        in_specs=[blk(0), blk(1), blk(2), g_spec, g_spec] + buf_specs,
        out_specs=(out, out, out, p_spec, s_spec, p_spec, s_spec),
        scratch_shapes=[pltpu.VMEM((ROW_TILE, w_a), F32)],
        input_output_aliases=aliases,
        compiler_params=_cparams(1),
        name="qk_norm",
    )(proj, proj, proj, gq, gk, *buf_args)
    return qn, kb, vb, (k_p, k_s, v_p, v_s)


def _stack_maps(q):
    lane = lax.broadcasted_iota(jnp.int32, q.shape, 1)
    zero = jnp.zeros_like(q)
    return jnp.where(lane < A_HEAD_DIM, q, zero), jnp.where(lane >= A_HEAD_DIM, q, zero)


def _attn_finish(acc, l, tq, lam, g):
    o = acc[:tq] / l[:tq] - lam * (acc[tq:] / l[tq:])
    return _rms(o) * g


def _attn_prompt_kernel(q_ref, k_ref, v_ref, g_ref, lam_ref, o_ref, qs_sc, m_sc, l_sc, acc_sc, *, tq):
    i = pl.program_id(1)
    q1, q2 = _stack_maps(q_ref[...])
    qs_sc[0:tq, :] = q1
    qs_sc[tq:2 * tq, :] = q2
    m_sc[...] = jnp.full(m_sc.shape, NEG_INF, F32)
    l_sc[...] = jnp.zeros(l_sc.shape, F32)
    acc_sc[...] = jnp.zeros(acc_sc.shape, F32)

    def step(j, masked):
        start = pl.multiple_of(j * tq, tq)
        kb = k_ref[pl.ds(start, tq), :]
        vb = v_ref[pl.ds(start, tq), :]
        s = lax.dot_general(qs_sc[...], kb, NT_DIMS, preferred_element_type=F32)
        if masked:
            row = lax.broadcasted_iota(jnp.int32, s.shape, 0)
            col = lax.broadcasted_iota(jnp.int32, s.shape, 1)
            s = jnp.where(col // CHUNK <= (row % tq) // CHUNK, s, NEG_INF)
        m_prev = m_sc[...]
        m_next = jnp.maximum(m_prev, jnp.max(s, axis=1, keepdims=True))
        p = jnp.exp2(s - m_next[:, 0:1])
        alpha = jnp.exp2(m_prev - m_next)
        l_sc[...] = alpha * l_sc[...] + jnp.sum(p, axis=1, keepdims=True)
        acc_sc[...] = alpha * acc_sc[...] + jnp.dot(p.astype(BF16), vb, preferred_element_type=F32)
        m_sc[...] = m_next

    def body(j, carry):
        step(j, False)
        return carry

    lax.fori_loop(0, i, body, 0)
    step(i, True)
    o_ref[...] = _attn_finish(acc_sc[...], l_sc[...], tq, lam_ref[...], g_ref[...]).astype(o_ref.dtype)


def _attn_prompt_shift_kernel(q_ref, k_ref, v_ref, sh_ref, g_ref, lam_ref, o_ref, qs_sc, acc_sc, *, tq):
    i = pl.program_id(1)
    q1, q2 = _stack_maps(q_ref[...])
    qs_sc[0:tq, :] = q1
    qs_sc[tq:2 * tq, :] = q2
    acc_sc[...] = jnp.zeros(acc_sc.shape, F32)
    shift = sh_ref[:, 0:1]
    ones = jnp.ones((tq, LANES), BF16)

    def body(j, carry):
        start = pl.multiple_of(j * tq, tq)
        kb = k_ref[pl.ds(start, tq), :]
        vb1 = jnp.concatenate([v_ref[pl.ds(start, tq), :], ones], axis=1)
        s = lax.dot_general(qs_sc[...], kb, NT_DIMS, preferred_element_type=F32)
        p = jnp.exp2(s - shift).astype(BF16)
        acc_sc[...] += jnp.dot(p, vb1, preferred_element_type=F32)
        return carry

    lax.fori_loop(0, i, body, 0)

    sub = ATTN_DIAG_SUB
    diag0 = pl.multiple_of(i * tq, tq)
    for j in range(tq // sub):
        k0 = pl.multiple_of(diag0 + j * sub, sub)
        kb = k_ref[pl.ds(k0, sub), :]
        vb1 = jnp.concatenate([v_ref[pl.ds(k0, sub), :], ones[:sub]], axis=1)
        for half in range(2):
            rows = slice(half * tq + j * sub, (half + 1) * tq)
            s = lax.dot_general(qs_sc[rows, :], kb, NT_DIMS, preferred_element_type=F32)
            row = lax.broadcasted_iota(jnp.int32, s.shape, 0)
            col = lax.broadcasted_iota(jnp.int32, s.shape, 1)
            s = jnp.where(col // CHUNK <= row // CHUNK, s, NEG_INF)
            p = jnp.exp2(s - shift).astype(BF16)
            acc_sc[rows, :] += jnp.dot(p, vb1, preferred_element_type=F32)

    acc = acc_sc[...]
    o_ref[...] = _attn_finish(acc[:, :A_VDIM], acc[:, A_VDIM:], tq, lam_ref[...], g_ref[...]).astype(o_ref.dtype)


def _attn_prompt_running_max(qn, kb, vb, g, lam, t, n_heads, out_rows):
    tq = ATTN_TILE
    kv_spec = pl.BlockSpec((t, A_VDIM), lambda h, i: (0, h))
    vec = pl.BlockSpec((1, LANES), lambda h, i: (0, 0))
    return pl.pallas_call(
        functools.partial(_attn_prompt_kernel, tq=tq),
        out_shape=jax.ShapeDtypeStruct((out_rows, n_heads * A_VDIM), BF16),
        grid=(n_heads, t // tq),
        in_specs=[pl.BlockSpec((tq, A_VDIM), lambda h, i: (i, h)), kv_spec, kv_spec, vec, vec],
        out_specs=pl.BlockSpec((tq, A_VDIM), lambda h, i: (i, h)),
        scratch_shapes=[pltpu.VMEM((2 * tq, A_VDIM), BF16), pltpu.VMEM((2 * tq, LANES), F32),
                        pltpu.VMEM((2 * tq, LANES), F32), pltpu.VMEM((2 * tq, A_VDIM), F32)],
        compiler_params=_cparams(2),
        name="attn_prompt",
    )(qn, kb, vb, g, lam)


def _attn_prompt_shift(qn, kb, vb, shift, g, lam, t, n_heads, out_rows):
    tq = ATTN_FAST_TILE
    kv_spec = pl.BlockSpec((t, A_VDIM), lambda h, i: (0, h))
    vec = pl.BlockSpec((1, LANES), lambda h, i: (0, 0))
    return pl.pallas_call(
        functools.partial(_attn_prompt_shift_kernel, tq=tq),
        out_shape=jax.ShapeDtypeStruct((out_rows, n_heads * A_VDIM), BF16),
        grid=(n_heads, t // tq),
        in_specs=[pl.BlockSpec((tq, A_VDIM), lambda h, i: (i, h)), kv_spec, kv_spec, vec, vec, vec],
        out_specs=pl.BlockSpec((tq, A_VDIM), lambda h, i: (i, h)),
        scratch_shapes=[pltpu.VMEM((2 * tq, A_VDIM), BF16), pltpu.VMEM((2 * tq, 2 * A_VDIM), F32)],
        compiler_params=_cparams(2),
        name="attn_prompt_shift",
    )(qn, kb, vb, shift, g, lam)


def _attn_prompt(qn, kb, vb, score_bound, g, lam, t, n_heads, out_rows):
    shift = jnp.full((1, LANES), score_bound, F32)
    return lax.cond(
        score_bound <= ATTN_SHIFT_MAX,
        lambda: _attn_prompt_shift(qn, kb, vb, shift, g, lam, t, n_heads, out_rows),
        lambda: _attn_prompt_running_max(qn, kb, vb, g, lam, t, n_heads, out_rows))


def _attn_sample_kernel(q_ref, k_ref, v_ref, g_ref, lam_ref, o_ref, *, tq, n_past, n_keys):
    q1, q2 = _stack_maps(q_ref[...])
    qs = jnp.concatenate([q1, q2], axis=0)
    s = lax.dot_general(qs, k_ref[0], NT_DIMS, preferred_element_type=F32)
    row = lax.broadcasted_iota(jnp.int32, s.shape, 0)
    col = lax.broadcasted_iota(jnp.int32, s.shape, 1)
    visible = (col // CHUNK <= (n_past + row % tq) // CHUNK) & (col < n_keys)
    s = jnp.where(visible, s, NEG_INF)
    p = jnp.exp2(s - jnp.max(s, axis=1, keepdims=True))
    p = jnp.where(col < n_keys, p, 0.0)
    l = jnp.sum(p, axis=1, keepdims=True)
    acc = jnp.dot(p.astype(BF16), v_ref[0], preferred_element_type=F32)
    o_ref[...] = _attn_finish(acc, l, tq, lam_ref[...], g_ref[...]).astype(o_ref.dtype)


def _into_buffers(kernel, n_inputs, bufs, out_indices):
    if bufs is None:
        return kernel, [], (), {}
    n_buf = len(bufs)

    def body(*refs):
        return kernel(*refs[:n_inputs], *refs[n_inputs + n_buf:])

    return (body, [pl.BlockSpec(memory_space=pl.ANY)] * n_buf, tuple(bufs),
            {n_inputs + j: o for j, o in enumerate(out_indices)})


def _into_buffer(kernel, n_inputs, out_buf):
    return _into_buffers(kernel, n_inputs, None if out_buf is None else (out_buf,), (0,))


def _attn_sample(qn, k_all, v_all, g, lam, out_buf, row0, n_batch, tq, n_heads, n_past, n_keys):
    kp = k_all.shape[1]
    q_blk0 = row0 // tq
    kv_spec = pl.BlockSpec((1, kp, A_VDIM), lambda b, h: (b, 0, h))
    vec = pl.BlockSpec((1, LANES), lambda b, h: (0, 0))
    row_spec = pl.BlockSpec((tq, A_VDIM), lambda b, h: (q_blk0 + b, h))
    body, buf_specs, buf_args, aliases = _into_buffer(
        functools.partial(_attn_sample_kernel, tq=tq, n_past=n_past, n_keys=n_keys), 5, out_buf)
    return pl.pallas_call(
        body,
        out_shape=jax.ShapeDtypeStruct(out_buf.shape, out_buf.dtype),
        grid=(n_batch, n_heads),
        in_specs=[row_spec, kv_spec, kv_spec, vec, vec] + buf_specs,
        out_specs=row_spec,
        input_output_aliases=aliases,
        compiler_params=_cparams(2),
        name="attn_sample",
    )(qn, k_all, v_all, g, lam, *buf_args)


def _hgrn_sum_matrix(L):
    t = np.arange(L)[:, None]
    s = np.arange(L)[None, :]
    mats = [s <= t]
    for lev in range(int(math.log2(L))):
        m = (t >> (lev + 1) << (lev + 1)) + (1 << lev) - 1
        mats.append(np.where(t > m, (s > m) & (s <= t), (s > t) & (s <= m)))
    return jnp.asarray(np.concatenate(mats, axis=0), BF16)


def _hgrn_kernel(q_ref, f_ref, i_ref, g_ref, m_ref, lb_ref, og_ref, s0_ref, o_ref, sfin_ref, s_sc, *, n_heads):
    c = pl.program_id(1)
    L = q_ref.shape[0]
    dk = B_HEAD_DIM

    @pl.when(c == 0)
    def _():
        s_sc[...] = s0_ref[0]

    qf = _silu(q_ref[...])
    lb = lb_ref[...]
    f = lb + (1.0 - lb) * _sigmoid(f_ref[...])
    kf = 1.0 - f
    vv = i_ref[...]
    gate = _silu(g_ref[...])

    lf_hi, lf_lo = _split(jnp.log(f))
    sums = (jnp.dot(m_ref[...], lf_hi, preferred_element_type=F32)
            + jnp.dot(m_ref[...], lf_lo, preferred_element_type=F32))
    b = sums[0:L]

    row = lax.broadcasted_iota(jnp.int32, (L, L), 0)
    col = lax.broadcasted_iota(jnp.int32, (L, L), 1)
    roww = lax.broadcasted_iota(jnp.int32, qf.shape, 0)
    atts = [jnp.zeros((L, L), F32) for _ in range(n_heads)]
    for lev in range(int(math.log2(L))):
        e = jnp.exp(sums[(lev + 1) * L:(lev + 2) * L])
        upper = ((roww >> lev) & 1) == 1
        qk = jnp.where(upper, qf * e, 0.0).astype(BF16)
        kk = jnp.where(upper, 0.0, kf * e).astype(BF16)
        same_block = (row >> (lev + 1)) == (col >> (lev + 1))
        for h in range(n_heads):
            sl = slice(h * dk, (h + 1) * dk)
            a = lax.dot_general(qk[:, sl], kk[:, sl], NT_DIMS, preferred_element_type=F32)
            atts[h] = atts[h] + jnp.where(same_block, a, 0.0)

    b_last = b[L - 1:L, :]
    qe = (qf * jnp.exp(b)).astype(BF16)
    kd = (kf * jnp.exp(b_last - b)).astype(BF16)
    vb = vv.astype(BF16)
    e_last = jnp.exp(b_last)
    eye = (lax.broadcasted_iota(jnp.int32, (dk, dk), 0) == lax.broadcasted_iota(jnp.int32, (dk, dk), 1))
    for h in range(n_heads):
        sl = slice(h * dk, (h + 1) * dk)
        att = atts[h] + jnp.where(row == col, jnp.sum(qf[:, sl] * kf[:, sl], axis=1, keepdims=True), 0.0)
        s_prev = s_sc[h]
        o = (jnp.dot(att.astype(BF16), vb[:, sl], preferred_element_type=F32)
             + jnp.dot(qe[:, sl], s_prev.astype(BF16), preferred_element_type=F32))
        decay = jnp.where(eye, jnp.broadcast_to(e_last[:, sl], (dk, dk)), 0.0)
        s_sc[h] = _dot3(decay, s_prev) + lax.dot_general(kd[:, sl], vb[:, sl], TN_DIMS, preferred_element_type=F32)
        o_ref[:, sl] = (_rms(o) * og_ref[...] * gate[:, sl]).astype(o_ref.dtype)
    sfin_ref[0] = s_sc[...]


def _hgrn(proj, lb, og, s0, out_buf, col0, row_blk0, n_batch, n_chunks, chunk, w_b):
    n_heads = w_b // B_HEAD_DIM
    cb = col0 // w_b
    sum_mat = _hgrn_sum_matrix(chunk)
    blk = lambda k: pl.BlockSpec((chunk, w_b), lambda b, c, k=k: (row_blk0 + b * n_chunks + c, cb + k))
    vec = lambda n: pl.BlockSpec((1, n), lambda b, c: (0, 0))
    st = pl.BlockSpec((1, n_heads, B_HEAD_DIM, B_HEAD_DIM), lambda b, c: (b, 0, 0, 0))
    body, buf_specs, buf_args, aliases = _into_buffer(functools.partial(_hgrn_kernel, n_heads=n_heads), 8, out_buf)
    return pl.pallas_call(
        body,
        out_shape=(jax.ShapeDtypeStruct((proj.shape[0], w_b), BF16), jax.ShapeDtypeStruct(s0.shape, F32)),
        grid=(n_batch, n_chunks),
        in_specs=[blk(0), blk(1), blk(2), blk(3), pl.BlockSpec(sum_mat.shape, lambda b, c: (0, 0)),
                  vec(w_b), vec(B_HEAD_DIM), st] + buf_specs,
        out_specs=(pl.BlockSpec((chunk, w_b), lambda b, c: (row_blk0 + b * n_chunks + c, 0)), st),
        scratch_shapes=[pltpu.VMEM((n_heads, B_HEAD_DIM, B_HEAD_DIM), F32)],
        input_output_aliases=aliases,
        compiler_params=_cparams(2),
        name="hgrn2",
    )(proj, proj, proj, proj, sum_mat, lb, og, s0, *buf_args)


def _gelu_tanh(x):
    return 0.5 * x * (1.0 + jnp.tanh(math.sqrt(2.0 / math.pi) * (x + 0.044715 * x * x * x)))


def _s5_kernel(u_ref, bre_ref, bim_ref, cre_ref, cim_ref, d_ref, tab_ref, x0re_ref, x0im_ref,
               glu_ref, og_ref, o_ref, fre_ref, fim_ref, xre_sc, xim_sc, cr_sc, ci_sc):
    t = pl.program_id(1)
    rows, w_c = u_ref.shape
    n_state = xre_sc.shape[1]
    n_blk = w_c // LANES
    sb = n_state // n_blk

    @pl.when(t == 0)
    def _():
        cr_sc[...] = jnp.broadcast_to(x0re_ref[0], cr_sc.shape)
        ci_sc[...] = jnp.broadcast_to(x0im_ref[0], ci_sc.shape)

    u = u_ref[...]
    for j in range(n_blk):
        uj = u[:, j * LANES:(j + 1) * LANES]
        xre_sc[:, j * sb:(j + 1) * sb] = _dot2(uj, bre_ref[j])
        xim_sc[:, j * sb:(j + 1) * sb] = _dot2(uj, bim_ref[j])

    half = n_state // 2

    def body(r, carry):
        r0 = pl.multiple_of(r * SUBLANES, SUBLANES)
        for hs in range(2):
            cs = slice(hs * half, (hs + 1) * half)
            xr = xre_sc[pl.ds(r0, SUBLANES), cs]
            xi = xim_sc[pl.ds(r0, SUBLANES), cs]
            for lev, shift in enumerate((1, 2, 4)):
                ar = tab_ref[2 * lev, :, cs]
                ai = tab_ref[2 * lev + 1, :, cs]
                sr = pltpu.roll(xr, shift, 0)
                si = pltpu.roll(xi, shift, 0)
                xr, xi = xr + (ar * sr - ai * si), xi + (ar * si + ai * sr)
            pr = tab_ref[6, :, cs]
            pi = tab_ref[7, :, cs]
            cr = cr_sc[:, cs]
            ci = ci_sc[:, cs]
            xr, xi = xr + (pr * cr - pi * ci), xi + (pr * ci + pi * cr)
            xre_sc[pl.ds(r0, SUBLANES), cs] = xr
            xim_sc[pl.ds(r0, SUBLANES), cs] = xi
            cr_sc[:, cs] = jnp.broadcast_to(xr[SUBLANES - 1:SUBLANES, :], (SUBLANES, half))
            ci_sc[:, cs] = jnp.broadcast_to(xi[SUBLANES - 1:SUBLANES, :], (SUBLANES, half))
        return carry

    lax.fori_loop(0, rows // SUBLANES, body, 0)

    ys = []
    for j in range(n_blk):
        ys.append(jnp.dot(xre_sc[:, j * sb:(j + 1) * sb].astype(BF16), cre_ref[j], preferred_element_type=F32)
                  - jnp.dot(xim_sc[:, j * sb:(j + 1) * sb].astype(BF16), cim_ref[j], preferred_element_type=F32))
    yc = _gelu_tanh(jnp.concatenate(ys, axis=1) + d_ref[...] * u)
    gl = jnp.dot(yc.astype(BF16), glu_ref[...], preferred_element_type=F32)
    v = gl[:, :w_c] * _sigmoid(gl[:, w_c:])
    o_ref[...] = (_rms(v) * og_ref[...]).astype(o_ref.dtype)
    fre_ref[0] = cr_sc[0:1, :]
    fim_ref[0] = ci_sc[0:1, :]


def _s5(proj, wts, x0re, x0im, out_buf, col0, row_blk0, n_batch, n_tiles, tile, w_c):
    bre, bim, cre, cim, dvec, tab, glu, og = wts
    n_state = tab.shape[2]
    cb = col0 // w_c
    full = lambda a: pl.BlockSpec(a.shape, lambda b, t, nd=a.ndim: (0,) * nd)
    st = pl.BlockSpec((1, 1, n_state), lambda b, t: (b, 0, 0))
    body, buf_specs, buf_args, aliases = _into_buffer(_s5_kernel, 11, out_buf)
    return pl.pallas_call(
        body,
        out_shape=(jax.ShapeDtypeStruct((proj.shape[0], w_c), BF16),
                   jax.ShapeDtypeStruct((n_batch, 1, n_state), F32),
                   jax.ShapeDtypeStruct((n_batch, 1, n_state), F32)),
        grid=(n_batch, n_tiles),
        in_specs=[pl.BlockSpec((tile, w_c), lambda b, t: (row_blk0 + b * n_tiles + t, cb)),
                  full(bre), full(bim), full(cre), full(cim), full(dvec), full(tab), st, st,
                  full(glu), full(og)] + buf_specs,
        out_specs=(pl.BlockSpec((tile, w_c), lambda b, t: (row_blk0 + b * n_tiles + t, 0)), st, st),
        scratch_shapes=[pltpu.VMEM((tile, n_state), F32), pltpu.VMEM((tile, n_state), F32),
                        pltpu.VMEM((SUBLANES, n_state), F32), pltpu.VMEM((SUBLANES, n_state), F32)],
        input_output_aliases=aliases,
        compiler_params=_cparams(2),
        name="s5",
    )(proj, bre, bim, cre, cim, dvec, tab, x0re, x0im, glu, og, *buf_args)


def _s5_weights(a_re, a_im, b_re, b_im, c_re, c_im, d, log_dt, glu_w, out_g):
    g, n = a_re.shape
    p = b_re.shape[2]
    lam_re = jnp.minimum(a_re.astype(F32), A_RE_MAX)
    lam_im = a_im.astype(F32)
    dt = jnp.exp(log_dt.astype(F32))[:, None]
    mag = jnp.exp(lam_re * dt)
    abar_re = mag * jnp.cos(lam_im * dt)
    abar_im = mag * jnp.sin(lam_im * dt)
    den = lam_re * lam_re + lam_im * lam_im
    z_re = abar_re - 1.0
    coef_re = (z_re * lam_re + abar_im * lam_im) / den
    coef_im = (abar_im * lam_re - z_re * lam_im) / den
    b_re = b_re.astype(F32)
    b_im = b_im.astype(F32)
    bbar_re = coef_re[..., None] * b_re - coef_im[..., None] * b_im
    bbar_im = coef_re[..., None] * b_im + coef_im[..., None] * b_re

    gpb = LANES // p
    n_blk = g // gpb
    eye = jnp.eye(gpb, dtype=F32)

    def blockdiag_in(bb):
        x = bb.reshape(n_blk, gpb, n, p)
        return jnp.einsum('jgnp,gh->jgphn', x, eye).reshape(n_blk, gpb * p, gpb * n)

    def blockdiag_out(cc):
        x = cc.astype(F32).reshape(n_blk, gpb, p, n)
        return jnp.einsum('jgpn,gh->jgnhp', x, eye).reshape(n_blk, gpb * n, gpb * p)

    pw_re, pw_im = [abar_re.reshape(-1)], [abar_im.reshape(-1)]
    for _ in range(SUBLANES - 1):
        pr, pi = pw_re[-1], pw_im[-1]
        pw_re.append(pr * pw_re[0] - pi * pw_im[0])
        pw_im.append(pr * pw_im[0] + pi * pw_re[0])
    rows = jnp.arange(SUBLANES)[:, None]
    tabs = []
    for k in (1, 2, 4):
        keep = rows >= k
        tabs.append(jnp.where(keep, pw_re[k - 1][None, :], 0.0))
        tabs.append(jnp.where(keep, pw_im[k - 1][None, :], 0.0))
    tabs.append(jnp.stack(pw_re))
    tabs.append(jnp.stack(pw_im))
    tab = jnp.stack(tabs)
    return (blockdiag_in(bbar_re).astype(BF16), blockdiag_in(bbar_im).astype(BF16),
            blockdiag_out(c_re).astype(BF16), blockdiag_out(c_im).astype(BF16),
            d.astype(F32).reshape(1, g * p), tab, glu_w.astype(BF16), out_g.astype(F32).reshape(1, -1))


def _take_rows(a, idx):
    return a.at[idx].get(mode="promise_in_bounds")


def _moe_plan(route, n_experts, m_pad):
    n_tok = route.shape[0]
    n = TOP_K * n_tok
    tm = MOE_ROW_TILE
    e_flat = jnp.concatenate([route[:, 0], route[:, 1]]).astype(jnp.int32)
    onehot = (e_flat[:, None] == jnp.arange(n_experts, dtype=jnp.int32)[None, :]).astype(jnp.int32)
    csum = jnp.cumsum(onehot, axis=0)
    counts = csum[-1]
    rank = jnp.sum(csum * onehot, axis=1) - 1
    padded = (counts + tm - 1) // tm * tm
    ends_pad = jnp.cumsum(padded)
    start_pad = ends_pad - padded
    start = jnp.cumsum(counts) - counts
    pos = jnp.sum(onehot * start_pad[None, :], axis=1) + rank
    order = jnp.argsort(e_flat, stable=True)
    n_tiles = m_pad // tm
    tile_start = jnp.arange(n_tiles, dtype=jnp.int32) * tm
    tile_expert = jnp.sum((tile_start[:, None] >= ends_pad[None, :]).astype(jnp.int32), axis=1)
    tile_expert = jnp.minimum(tile_expert, jnp.max(e_flat))
    k = jnp.arange(m_pad, dtype=jnp.int32) - jnp.repeat(start_pad[tile_expert], tm)
    valid = k < jnp.repeat(counts[tile_expert], tm)
    slot = jnp.clip(jnp.repeat(start[tile_expert], tm) + k, 0, n - 1)
    row_token = jnp.where(valid, _take_rows(order, slot) % n_tok, 0)
    n_valid = (ends_pad[-1] // tm).astype(jnp.int32)
    return row_token, pos[:n_tok], pos[n_tok:], tile_expert, n_valid.reshape(1)


def _cast_pad_kernel(w_ref, o_ref):
    n = w_ref.shape[1]
    o_ref[:, :n] = w_ref[...].astype(o_ref.dtype)
    if o_ref.shape[1] > n:
        o_ref[:, n:] = jnp.zeros((o_ref.shape[0], o_ref.shape[1] - n), o_ref.dtype)


def _cast_pad_cols(w, n_pad):
    lead, n = w.shape[:-1], w.shape[-1]
    rows = math.prod(lead)
    tr = max(r for r in (1024, 512, 256, 128, 64, 32, 16)
             if rows % r == 0 and (r * n * 4 <= CAST_BLOCK_BYTES or r == 16))
    out = pl.pallas_call(
        _cast_pad_kernel,
        out_shape=jax.ShapeDtypeStruct((rows, n_pad), BF16),
        grid=(rows // tr,),
        in_specs=[pl.BlockSpec((tr, n), lambda i: (i, 0))],
        out_specs=pl.BlockSpec((tr, n_pad), lambda i: (i, 0)),
        compiler_params=_cparams(1),
        name="cast_pad",
    )(w.reshape(rows, n))
    return out.reshape(lead + (n_pad,))


def kernel(x_prompt, x_sample, c_prompt, c_sample, cache_k, cache_v, state_hgrn, state_ssm_re, state_ssm_im, w_ada, b_ada, norm1_g, norm2_g, w_in, w_out, q_norm_g, k_norm_g, lambda_qk, attn_out_g, hgrn_lower_bounds, hgrn_out_g, ssm_a_re, ssm_a_im, ssm_b_re, ssm_b_im, ssm_c_re, ssm_c_im, ssm_d, ssm_log_dt, ssm_glu_w, ssm_out_g, ffn_w1, ffn_w3, ffn_w2, moe_router, moe_w1, moe_w3, moe_w2):
    bp, t, d = x_prompt.shape
    bs, ts, _ = x_sample.shape
    depth = w_in.shape[0]
    n_past = cache_k.shape[2]
    n_heads = cache_k.shape[3]
    w_a = n_heads * A_VDIM
    w_b = hgrn_lower_bounds.shape[1]
    w_c = ssm_out_g.shape[1]
    n_state = ssm_a_re.shape[1] * ssm_a_re.shape[2]
    n_sample = bs * ts
    assert bp == 1 and t % ROW_TILE == 0 and t % ATTN_TILE == 0 and t % ATTN_FAST_TILE == 0
    assert t % S5_TILE == 0 and t % HGRN_TILE == 0
    assert n_sample % ROW_TILE == 0 and ROW_TILE % ts == 0 and ts <= CHUNK and ts % 16 == 0
    assert (ts & (ts - 1)) == 0 and w_a == 2 * w_b and w_b == w_c
    n_pt = t // ROW_TILE
    m_tot = t + n_sample

    x = jnp.concatenate([x_prompt.reshape(t, d), x_sample.reshape(n_sample, d)], axis=0)

    c_rows = jnp.concatenate([c_prompt, c_sample], axis=0)
    c_rows = jnp.pad(c_rows, ((0, _round_up(1 + bs, 16) - (1 + bs)), (0, 0)))
    mod = _modulation(c_rows.astype(F32), w_ada, b_ada)

    def mod_blocks(l, k):
        m = mod[l, :, k * d:(k + 1) * d]
        return jnp.concatenate([jnp.broadcast_to(m[0:1], (ROW_TILE, d)),
                                jnp.repeat(m[1:1 + bs], ts, axis=0)], axis=0)

    lb_p = jax.nn.softmax(hgrn_lower_bounds.astype(F32), axis=0)
    lower_bounds = jnp.cumsum(lb_p, axis=0) - lb_p[0]

    zero_hgrn = jnp.zeros((bp,) + state_hgrn.shape[2:], F32)
    zero_ssm = jnp.zeros((bp, 1, n_state), F32)
    n_keys = n_past + ts
    keys_pad = _round_up(n_keys, LANES)

    sp, rep, imp = [], [], []
    ss, res, ims = [], [], []
    kv_out = None
    y_split = None
    for l in range(depth):
        sh1, sc1, g1, sh2, sc2, g2 = [mod_blocks(l, k) for k in range(6)]
        h = _norm_mod(x, norm1_g[l].reshape(1, d), sc1, sh1, n_pt)
        proj = _matmul(h, _cast_pad_cols(w_in[l], w_in.shape[2]))

        lam_init = 0.8 - 0.6 * math.exp(-0.3 * l)
        lq = lambda_qk[l].astype(F32)
        lam = jnp.exp(jnp.sum(lq[0] * lq[1])) - jnp.exp(jnp.sum(lq[2] * lq[3])) + lam_init
        lam_vec = jnp.full((1, LANES), lam, F32)
        og_a = (attn_out_g[l].astype(F32) * (1.0 - lam_init)).reshape(1, A_VDIM)
        gq = jnp.tile(q_norm_g[l].astype(F32), 2).reshape(1, LANES)
        gk = jnp.tile(k_norm_g[l].astype(F32), 2).reshape(1, LANES)
        qn, kb, vb, kv_out = _qknorm(proj, gq, gk, w_a, l, depth, n_pt, kv_out)
        score_bound = 1.02 * A_HEAD_DIM * Q_SCALE * jnp.max(jnp.abs(gq)) * jnp.max(jnp.abs(gk))
        oa = _attn_prompt(qn, kb, vb, score_bound, og_a, lam_vec, t, n_heads, m_tot)
        pad = jnp.zeros((bs, keys_pad - n_keys, w_a), BF16)
        k_all = jnp.concatenate([cache_k[l].reshape(bs, n_past, w_a).astype(BF16),
                                 kb[t:].reshape(bs, ts, w_a), pad], axis=1)
        v_all = jnp.concatenate([cache_v[l].reshape(bs, n_past, w_a).astype(BF16),
                                 vb[t:].reshape(bs, ts, w_a), pad], axis=1)
        oa = _attn_sample(qn, k_all, v_all, og_a, lam_vec, oa, t, bs, ts, n_heads, n_past, n_keys)

        lb = lower_bounds[l].reshape(1, w_b)
        og_b = hgrn_out_g[l].astype(F32).reshape(1, B_HEAD_DIM)
        col_b = 3 * w_a
        ob, s_p = _hgrn(proj, lb, og_b, zero_hgrn, None, col_b, 0, bp, t // HGRN_TILE, HGRN_TILE, w_b)
        ob, s_s = _hgrn(proj, lb, og_b, state_hgrn[l].astype(F32), ob, col_b, t // ts, bs, 1, ts, w_b)

        wts = _s5_weights(ssm_a_re[l], ssm_a_im[l], ssm_b_re[l], ssm_b_im[l], ssm_c_re[l], ssm_c_im[l],
                          ssm_d[l], ssm_log_dt[l], ssm_glu_w[l], ssm_out_g[l])
        col_c = 3 * w_a + 4 * w_b
        oc, re_p, im_p = _s5(proj, wts, zero_ssm, zero_ssm, None, col_c, 0, bp, t // S5_TILE, S5_TILE, w_c)
        oc, re_s, im_s = _s5(proj, wts, state_ssm_re[l].astype(F32).reshape(bs, 1, n_state),
                             state_ssm_im[l].astype(F32).reshape(bs, 1, n_state),
                             oc, col_c, t // ts, bs, 1, ts, w_c)

        x = _matmul_residual((oa, ob, oc), _cast_pad_cols(w_out[l], d), x, g1, n_pt, tn_target=1024)

        if l % 2 == 0:
            j = l // 2
            dff = ffn_w1.shape[2]
            dff_pad = _round_up(dff, 512)
            h2 = _norm_mod(x, norm2_g[l].reshape(1, d), sc2, sh2, n_pt)
            a = _swiglu_up(h2, _cast_pad_cols(ffn_w1[j], dff_pad), _cast_pad_cols(ffn_w3[j], dff_pad))
            x = _matmul_residual(a, _cast_pad_cols(ffn_w2[j], d), x, g2, n_pt)
        else:
            j = l // 2
            n_experts = moe_router.shape[2]
            dff = moe_w1.shape[3]
            dff_pad = _round_up(dff, 512)
            h2, route = _norm_mod(x, norm2_g[l].reshape(1, d), sc2, sh2, n_pt, w_router=moe_router[j])
            m_pad = TOP_K * m_tot + n_experts * MOE_ROW_TILE
            row_token, pos0, pos1, tile_expert, n_valid = _moe_plan(route, n_experts, m_pad)
            up = _grouped_swiglu_up(_take_rows(h2, row_token), _cast_pad_cols(moe_w1[j], dff_pad),
                                    _cast_pad_cols(moe_w3[j], dff_pad), tile_expert, n_valid)
            down = _grouped_down(up, moe_w2[j], tile_expert, n_valid)
            r0, r1 = _take_rows(down, pos0), _take_rows(down, pos1)
            if l == depth - 1:
                y_split = _moe_combine(x, r0, r1, route, g2, n_pt, split=True)
            else:
                x = _moe_combine(x, r0, r1, route, g2, n_pt)

        sp.append(s_p)
        rep.append(re_p.reshape(bp, -1, C_STATE))
        imp.append(im_p.reshape(bp, -1, C_STATE))
        ss.append(s_s)
        res.append(re_s.reshape(bs, -1, C_STATE))
        ims.append(im_s.reshape(bs, -1, C_STATE))

    y_p, y_s = y_split if y_split is not None else (x[:t], x[t:])
    k_p, k_s, v_p, v_s = kv_out
    return (y_p.reshape(bp, t, d), y_s.reshape(bs, ts, d),
            k_p.reshape(depth, bp, t, n_heads, 2, A_HEAD_DIM), v_p.reshape(depth, bp, t, n_heads, A_VDIM),
            jnp.stack(sp), jnp.stack(rep), jnp.stack(imp),
            k_s.reshape(depth, bs, ts, n_heads, 2, A_HEAD_DIM), v_s.reshape(depth, bs, ts, n_heads, A_VDIM),
            jnp.stack(ss), jnp.stack(res), jnp.stack(ims))
```

```python
import functools
import math

import numpy as np
import jax
import jax.numpy as jnp
from jax import lax
from jax.experimental import pallas as pl
from jax.experimental.pallas import tpu as pltpu

F32 = jnp.float32
BF16 = jnp.bfloat16

CHUNK = 64
A_HEAD_DIM = 64
A_VDIM = 2 * A_HEAD_DIM
B_HEAD_DIM = 128
C_GROUP = 16
C_STATE = 64
TOP_K = 2
RMS_EPS = 1e-6
NEG_INF = -1e30
A_RE_MAX = -1e-4

LANES = 128
SUBLANES = 8
ROW_TILE = 512
MOE_ROW_TILE = 512
ATTN_TILE = 512
ATTN_FAST_TILE = 1024
ATTN_DIAG_SUB = 256
ATTN_SHIFT_MAX = 60.0
HGRN_TILE = 256
MOE_UP_COLS = 512
COMBINE_COLS = 1024
S5_TILE = 512
VMEM_LIMIT = 48 * 1024 * 1024
CAST_BLOCK_BYTES = 6 * 1024 * 1024

Q_SCALE = (A_HEAD_DIM ** -0.5) * math.log2(math.e)

NT_DIMS = (((1,), (1,)), ((), ()))
TN_DIMS = (((0,), (0,)), ((), ()))
NN_DIMS = (((1,), (0,)), ((), ()))


def _cparams(n_axes):
    return pltpu.CompilerParams(
        dimension_semantics=("arbitrary",) * n_axes, vmem_limit_bytes=VMEM_LIMIT)


def _round_up(x, m):
    return (x + m - 1) // m * m


def _pick_tile(n, target):
    best = LANES
    for t in range(LANES, min(n, target) + 1, LANES):
        if n % t == 0:
            best = t
    return best


def _split(x):
    hi = x.astype(BF16)
    lo = (x - hi.astype(F32)).astype(BF16)
    return hi, lo


def _dot3(a, b, dims=NN_DIMS):
    ah, al = _split(a)
    bh, bl = _split(b)
    d = functools.partial(lax.dot_general, dimension_numbers=dims, preferred_element_type=F32)
    return d(ah, bh) + d(al, bh) + d(ah, bl)


def _dot2(a, w):
    ah, al = _split(a)
    return jnp.dot(ah, w, preferred_element_type=F32) + jnp.dot(al, w, preferred_element_type=F32)


def _sigmoid(x):
    return 1.0 / (1.0 + jnp.exp(-x))


def _silu(x):
    return x * _sigmoid(x)


def _rms(x):
    return x * lax.rsqrt(jnp.mean(x * x, axis=-1, keepdims=True) + RMS_EPS)


def _mod_kernel(c_ref, w_ref, b_ref, o_ref):
    o_ref[0] = _dot3(_silu(c_ref[...]), w_ref[0]) + b_ref[0]


def _modulation(c_rows, w_ada, b_ada):
    depth, d, n = w_ada.shape
    rows = c_rows.shape[0]
    tn = _pick_tile(n, 1024)
    return pl.pallas_call(
        _mod_kernel,
        out_shape=jax.ShapeDtypeStruct((depth, rows, n), F32),
        grid=(depth, n // tn),
        in_specs=[
            pl.BlockSpec((rows, d), lambda l, j: (0, 0)),
            pl.BlockSpec((1, d, tn), lambda l, j: (l, 0, j)),
            pl.BlockSpec((1, 1, tn), lambda l, j: (l, 0, j)),
        ],
        out_specs=pl.BlockSpec((1, rows, tn), lambda l, j: (l, 0, j)),
        compiler_params=_cparams(2),
        name="adaln_mod",
    )(c_rows, w_ada, b_ada.reshape(depth, 1, n))


def _norm_mod_kernel(x_ref, g_ref, sc_ref, sh_ref, h_ref):
    y = _rms(x_ref[...]) * g_ref[...]
    h_ref[...] = (y * (1.0 + sc_ref[...]) + sh_ref[...]).astype(h_ref.dtype)


def _norm_mod_route_kernel(x_ref, g_ref, sc_ref, sh_ref, wr_ref, h_ref, r_ref, *, n_experts):
    y = _rms(x_ref[...]) * g_ref[...]
    h = y * (1.0 + sc_ref[...]) + sh_ref[...]
    h_ref[...] = h.astype(h_ref.dtype)
    logits = _dot3(h, wr_ref[...])
    lane = lax.broadcasted_iota(jnp.int32, logits.shape, 1).astype(F32)
    lg = jnp.where(lane < n_experts, logits, -jnp.inf)
    m1 = jnp.max(lg, axis=1, keepdims=True)
    i1 = jnp.min(jnp.where(lg == m1, lane, float(LANES)), axis=1, keepdims=True)
    lg2 = jnp.where(lane == i1, -jnp.inf, lg)
    m2 = jnp.max(lg2, axis=1, keepdims=True)
    i2 = jnp.min(jnp.where(lg2 == m2, lane, float(LANES)), axis=1, keepdims=True)
    e = jnp.exp(m2 - m1)
    inv = 1.0 / (1.0 + e)
    r_ref[...] = jnp.where(lane == 0.0, i1, jnp.where(lane == 1.0, i2, jnp.where(
        lane == 2.0, inv, jnp.where(lane == 3.0, e * inv, 0.0))))


def _mod_index(n_prompt_tiles):
    return lambda i: (jnp.maximum(i - n_prompt_tiles + 1, 0), 0)


def _norm_mod(x, g, sc, sh, n_prompt_tiles, w_router=None):
    m, d = x.shape
    mi = _mod_index(n_prompt_tiles)
    in_specs = [
        pl.BlockSpec((ROW_TILE, d), lambda i: (i, 0)),
        pl.BlockSpec((1, d), lambda i: (0, 0)),
        pl.BlockSpec((ROW_TILE, d), mi),
        pl.BlockSpec((ROW_TILE, d), mi),
    ]
    h_spec = pl.BlockSpec((ROW_TILE, d), lambda i: (i, 0))
    if w_router is None:
        return pl.pallas_call(
            _norm_mod_kernel,
            out_shape=jax.ShapeDtypeStruct((m, d), BF16),
            grid=(m // ROW_TILE,),
            in_specs=in_specs,
            out_specs=h_spec,
            compiler_params=_cparams(1),
            name="norm_mod",
        )(x, g, sc, sh)
    n_experts = w_router.shape[1]
    wr = jnp.zeros((d, LANES), F32).at[:, :n_experts].set(w_router.astype(F32))
    return pl.pallas_call(
        functools.partial(_norm_mod_route_kernel, n_experts=n_experts),
        out_shape=(jax.ShapeDtypeStruct((m, d), BF16), jax.ShapeDtypeStruct((m, LANES), F32)),
        grid=(m // ROW_TILE,),
        in_specs=in_specs + [pl.BlockSpec((d, LANES), lambda i: (0, 0))],
        out_specs=(h_spec, pl.BlockSpec((ROW_TILE, LANES), lambda i: (i, 0))),
        compiler_params=_cparams(1),
        name="norm_mod_route",
    )(x, g, sc, sh, wr)


def _mm_kernel(a_ref, w_ref, o_ref):
    o_ref[...] = jnp.dot(a_ref[...], w_ref[...], preferred_element_type=F32).astype(o_ref.dtype)


def _mm_res_kernel(*refs, widths):
    a_refs, (w_ref, x_ref, g_ref, o_ref) = refs[:len(widths)], refs[len(widths):]
    acc, k0 = None, 0
    for a_ref, kw in zip(a_refs, widths):
        part = jnp.dot(a_ref[...], w_ref[k0:k0 + kw, :], preferred_element_type=F32)
        acc = part if acc is None else acc + part
        k0 += kw
    o_ref[...] = x_ref[...] + g_ref[...] * acc


def _swiglu_up_kernel(a_ref, w1_ref, w3_ref, o_ref):
    a = a_ref[...]
    u = jnp.dot(a, w1_ref[...], preferred_element_type=F32)
    v = jnp.dot(a, w3_ref[...], preferred_element_type=F32)
    o_ref[...] = (_silu(u) * v).astype(o_ref.dtype)


def _matmul(a, w, tn_target=1408):
    m, k = a.shape
    n = w.shape[1]
    tn = _pick_tile(n, tn_target)
    return pl.pallas_call(
        _mm_kernel,
        out_shape=jax.ShapeDtypeStruct((m, n), F32),
        grid=(n // tn, m // ROW_TILE),
        in_specs=[
            pl.BlockSpec((ROW_TILE, k), lambda j, i: (i, 0)),
            pl.BlockSpec((k, tn), lambda j, i: (0, j)),
        ],
        out_specs=pl.BlockSpec((ROW_TILE, tn), lambda j, i: (i, j)),
        compiler_params=_cparams(2),
        name="matmul",
    )(a, w)


def _matmul_residual(a, w, x, gate, n_prompt_tiles, tn_target=512):
    k, n = w.shape
    parts = a if isinstance(a, tuple) else (a,)
    widths = tuple(p.shape[1] for p in parts) if isinstance(a, tuple) else (k,)
    assert sum(widths) == k
    m = parts[0].shape[0]
    tn = _pick_tile(n, tn_target)
    return pl.pallas_call(
        functools.partial(_mm_res_kernel, widths=widths),
        out_shape=jax.ShapeDtypeStruct((m, n), F32),
        grid=(n // tn, m // ROW_TILE),
        in_specs=[pl.BlockSpec((ROW_TILE, kw), lambda j, i: (i, 0)) for kw in widths] + [
            pl.BlockSpec((k, tn), lambda j, i: (0, j)),
            pl.BlockSpec((ROW_TILE, tn), lambda j, i: (i, j)),
            pl.BlockSpec((ROW_TILE, tn), lambda j, i: (jnp.maximum(i - n_prompt_tiles + 1, 0), j)),
        ],
        out_specs=pl.BlockSpec((ROW_TILE, tn), lambda j, i: (i, j)),
        compiler_params=_cparams(2),
        name="matmul_residual",
    )(*parts, w, x, gate)


def _swiglu_up(a, w1, w3, tn_target=1408):
    m, k = a.shape
    n = w1.shape[1]
    tn = _pick_tile(n, tn_target)
    return pl.pallas_call(
        _swiglu_up_kernel,
        out_shape=jax.ShapeDtypeStruct((m, n), BF16),
        grid=(n // tn, m // ROW_TILE),
        in_specs=[
            pl.BlockSpec((ROW_TILE, k), lambda j, i: (i, 0)),
            pl.BlockSpec((k, tn), lambda j, i: (0, j)),
            pl.BlockSpec((k, tn), lambda j, i: (0, j)),
        ],
        out_specs=pl.BlockSpec((ROW_TILE, tn), lambda j, i: (i, j)),
        compiler_params=_cparams(2),
        name="swiglu_up",
    )(a, w1, w3)


def _g_up_kernel(te_ref, nv_ref, a_ref, w1_ref, w3_ref, o_ref, w1b_sc, w3b_sc, *, n_cols):
    j = pl.program_id(0)
    i = pl.program_id(1)

    @pl.when((i == 0) | (te_ref[i] != te_ref[jnp.maximum(i - 1, 0)]))
    def _():
        col = lax.broadcasted_iota(jnp.int32, w1b_sc.shape, 1) + j * w1b_sc.shape[1]
        w1b_sc[...] = jnp.where(col < n_cols, w1_ref[0], 0.0).astype(BF16)
        w3b_sc[...] = jnp.where(col < n_cols, w3_ref[0], 0.0).astype(BF16)

    @pl.when(i < nv_ref[0])
    def _():
        a = a_ref[...]
        u = jnp.dot(a, w1b_sc[...], preferred_element_type=F32)
        v = jnp.dot(a, w3b_sc[...], preferred_element_type=F32)
        o_ref[...] = (_silu(u) * v).astype(o_ref.dtype)

    @pl.when(i >= nv_ref[0])
    def _():
        o_ref[...] = jnp.zeros_like(o_ref)


def _g_down_kernel(te_ref, nv_ref, a_ref, w_ref, o_ref, wb_sc):
    i = pl.program_id(1)

    @pl.when((i == 0) | (te_ref[i] != te_ref[jnp.maximum(i - 1, 0)]))
    def _():
        wb_sc[...] = w_ref[0].astype(BF16)

    @pl.when(i < nv_ref[0])
    def _():
        o_ref[...] = jnp.dot(a_ref[...], wb_sc[...], preferred_element_type=F32)

    @pl.when(i >= nv_ref[0])
    def _():
        o_ref[...] = jnp.zeros_like(o_ref)


def _grouped_swiglu_up(a, w1, w3, tile_expert, n_valid):
    m, k = a.shape
    n = w1.shape[2]
    tn = MOE_UP_COLS
    n_out = _round_up(n, tn)
    tm = MOE_ROW_TILE
    w_spec = pl.BlockSpec((1, k, tn), lambda j, i, te, nv: (te[i], 0, j))
    grid_spec = pltpu.PrefetchScalarGridSpec(
        num_scalar_prefetch=2,
        grid=(n_out // tn, m // tm),
        in_specs=[pl.BlockSpec((tm, k), lambda j, i, te, nv: (i, 0)), w_spec, w_spec],
        out_specs=pl.BlockSpec((tm, tn), lambda j, i, te, nv: (i, j)),
        scratch_shapes=[pltpu.VMEM((k, tn), BF16), pltpu.VMEM((k, tn), BF16)],
    )
    return pl.pallas_call(
        functools.partial(_g_up_kernel, n_cols=n),
        out_shape=jax.ShapeDtypeStruct((m, n_out), BF16),
        grid_spec=grid_spec,
        compiler_params=_cparams(2),
        name="moe_swiglu_up",
    )(tile_expert, n_valid, a, w1.astype(F32), w3.astype(F32))


def _grouped_down(a, w, tile_expert, n_valid, tn_target=512):
    m = a.shape[0]
    _, k, n = w.shape
    tn = _pick_tile(n, tn_target)
    tm = MOE_ROW_TILE
    grid_spec = pltpu.PrefetchScalarGridSpec(
        num_scalar_prefetch=2,
        grid=(n // tn, m // tm),
        in_specs=[
            pl.BlockSpec((tm, k), lambda j, i, te, nv: (i, 0)),
            pl.BlockSpec((1, k, tn), lambda j, i, te, nv: (te[i], 0, j)),
        ],
        out_specs=pl.BlockSpec((tm, tn), lambda j, i, te, nv: (i, j)),
        scratch_shapes=[pltpu.VMEM((k, tn), BF16)],
    )
    return pl.pallas_call(
        _g_down_kernel,
        out_shape=jax.ShapeDtypeStruct((m, n), F32),
        grid_spec=grid_spec,
        compiler_params=_cparams(2),
        name="moe_down",
    )(tile_expert, n_valid, a, w.astype(F32))


def _moe_combine_kernel(x_ref, r0_ref, r1_ref, rt_ref, g_ref, o_ref):
    rt = rt_ref[...]
    y = rt[:, 2:3] * r0_ref[...] + rt[:, 3:4] * r1_ref[...]
    o_ref[...] = x_ref[...] + g_ref[...] * y


def _moe_combine_split_kernel(x_ref, r0_ref, r1_ref, rt_ref, g_ref, op_ref, os_ref, *, n_prompt_tiles):
    i = pl.program_id(1)
    rt = rt_ref[...]
    y = x_ref[...] + g_ref[...] * (rt[:, 2:3] * r0_ref[...] + rt[:, 3:4] * r1_ref[...])

    @pl.when(i < n_prompt_tiles)
    def _():
        op_ref[...] = y

    @pl.when(i >= n_prompt_tiles)
    def _():
        os_ref[...] = y


def _moe_combine(x, r0, r1, route, gate, n_prompt_tiles, split=False):
    m, d = x.shape
    tc = _pick_tile(d, COMBINE_COLS)
    row = pl.BlockSpec((ROW_TILE, tc), lambda c, i: (i, c))
    in_specs = [row, row, row, pl.BlockSpec((ROW_TILE, LANES), lambda c, i: (i, 0)),
                pl.BlockSpec((ROW_TILE, tc), lambda c, i: (jnp.maximum(i - n_prompt_tiles + 1, 0), c))]
    grid = (d // tc, m // ROW_TILE)
    if not split:
        return pl.pallas_call(
            _moe_combine_kernel,
            out_shape=jax.ShapeDtypeStruct((m, d), F32),
            grid=grid,
            in_specs=in_specs,
            out_specs=row,
            compiler_params=_cparams(2),
            name="moe_combine",
        )(x, r0, r1, route, gate)
    t = n_prompt_tiles * ROW_TILE
    return pl.pallas_call(
        functools.partial(_moe_combine_split_kernel, n_prompt_tiles=n_prompt_tiles),
        out_shape=(jax.ShapeDtypeStruct((t, d), F32), jax.ShapeDtypeStruct((m - t, d), F32)),
        grid=grid,
        in_specs=in_specs,
        out_specs=(pl.BlockSpec((ROW_TILE, tc), lambda c, i: (jnp.minimum(i, n_prompt_tiles - 1), c)),
                   pl.BlockSpec((ROW_TILE, tc), lambda c, i: (jnp.maximum(i - n_prompt_tiles, 0), c))),
        compiler_params=_cparams(2),
        name="moe_combine_split",
    )(x, r0, r1, route, gate)


def _qknorm_kernel(q_ref, k_ref, v_ref, gq_ref, gk_ref, qn_ref, kb_ref, vb_ref, kp_ref, ks_ref, vp_ref, vs_ref,
                   kn_sc, *, n_prompt_tiles):
    i = pl.program_id(0)
    r = lax.broadcasted_iota(jnp.int32, (LANES, LANES), 0) // A_HEAD_DIM
    c = lax.broadcasted_iota(jnp.int32, (LANES, LANES), 1) // A_HEAD_DIM
    seg = jnp.where(r == c, 1.0 / A_HEAD_DIM, 0.0).astype(BF16)

    def norm(x, g):
        hi, lo = _split(x * x)
        ms = (jnp.dot(hi, seg, preferred_element_type=F32)
              + jnp.dot(lo, seg, preferred_element_type=F32))
        return x * lax.rsqrt(ms + RMS_EPS) * g

    for h in range(q_ref.shape[1] // LANES):
        sl = slice(h * LANES, (h + 1) * LANES)
        qn = norm(q_ref[:, sl], gq_ref[...])
        qn_ref[:, sl] = (qn * Q_SCALE).astype(BF16)
        kn = norm(k_ref[:, sl], gk_ref[...])
        kn_sc[:, sl] = kn
        kb_ref[:, sl] = kn.astype(BF16)
    vb_ref[...] = v_ref[...].astype(BF16)

    @pl.when(i < n_prompt_tiles)
    def _():
        kp_ref[0] = kn_sc[...]
        vp_ref[0] = v_ref[...]

    @pl.when(i >= n_prompt_tiles)
    def _():
        ks_ref[0] = kn_sc[...]
        vs_ref[0] = v_ref[...]


def _qknorm(proj, gq, gk, w_a, layer, depth, n_prompt_tiles, kv_bufs):
    m = proj.shape[0]
    t = n_prompt_tiles * ROW_TILE
    blk = lambda c: pl.BlockSpec((ROW_TILE, w_a), lambda i, c=c: (i, c))
    g_spec = pl.BlockSpec((1, LANES), lambda i: (0, 0))
    out = pl.BlockSpec((ROW_TILE, w_a), lambda i: (i, 0))
    p_spec = pl.BlockSpec((1, ROW_TILE, w_a), lambda i: (layer, jnp.minimum(i, n_prompt_tiles - 1), 0))
    s_spec = pl.BlockSpec((1, ROW_TILE, w_a), lambda i: (layer, jnp.maximum(i - n_prompt_tiles, 0), 0))
    body, buf_specs, buf_args, aliases = _into_buffers(
        functools.partial(_qknorm_kernel, n_prompt_tiles=n_prompt_tiles), 5, kv_bufs, (3, 4, 5, 6))
    bf = jax.ShapeDtypeStruct((m, w_a), BF16)
    kv_p = jax.ShapeDtypeStruct((depth, t, w_a), F32)
    kv_s = jax.ShapeDtypeStruct((depth, m - t, w_a), F32)
    qn, kb, vb, k_p, k_s, v_p, v_s = pl.pallas_call(
        body,
        out_shape=(bf, bf, bf, kv_p, kv_s, kv_p, kv_s),
        grid=(m // ROW_TILE,),
        in_specs=[blk(0), blk(1), blk(2), g_spec, g_spec] + buf_specs,
        out_specs=(out, out, out, p_spec, s_spec, p_spec, s_spec),
        scratch_shapes=[pltpu.VMEM((ROW_TILE, w_a), F32)],
        input_output_aliases=aliases,
        compiler_params=_cparams(1),
        name="qk_norm",
    )(proj, proj, proj, gq, gk, *buf_args)
    return qn, kb, vb, (k_p, k_s, v_p, v_s)


def _stack_maps(q):
    lane = lax.broadcasted_iota(jnp.int32, q.shape, 1)
    zero = jnp.zeros_like(q)
    return jnp.where(lane < A_HEAD_DIM, q, zero), jnp.where(lane >= A_HEAD_DIM, q, zero)


def _attn_finish(acc, l, tq, lam, g):
    o = acc[:tq] / l[:tq] - lam * (acc[tq:] / l[tq:])
    return _rms(o) * g


def _attn_prompt_kernel(q_ref, k_ref, v_ref, g_ref, lam_ref, o_ref, qs_sc, m_sc, l_sc, acc_sc, *, tq):
    i = pl.program_id(1)
    q1, q2 = _stack_maps(q_ref[...])
    qs_sc[0:tq, :] = q1
    qs_sc[tq:2 * tq, :] = q2
    m_sc[...] = jnp.full(m_sc.shape, NEG_INF, F32)
    l_sc[...] = jnp.zeros(l_sc.shape, F32)
    acc_sc[...] = jnp.zeros(acc_sc.shape, F32)

    def step(j, masked):
        start = pl.multiple_of(j * tq, tq)
        kb = k_ref[pl.ds(start, tq), :]
        vb = v_ref[pl.ds(start, tq), :]
        s = lax.dot_general(qs_sc[...], kb, NT_DIMS, preferred_element_type=F32)
        if masked:
            row = lax.broadcasted_iota(jnp.int32, s.shape, 0)
            col = lax.broadcasted_iota(jnp.int32, s.shape, 1)
            s = jnp.where(col // CHUNK <= (row % tq) // CHUNK, s, NEG_INF)
        m_prev = m_sc[...]
        m_next = jnp.maximum(m_prev, jnp.max(s, axis=1, keepdims=True))
        p = jnp.exp2(s - m_next[:, 0:1])
        alpha = jnp.exp2(m_prev - m_next)
        l_sc[...] = alpha * l_sc[...] + jnp.sum(p, axis=1, keepdims=True)
        acc_sc[...] = alpha * acc_sc[...] + jnp.dot(p.astype(BF16), vb, preferred_element_type=F32)
        m_sc[...] = m_next

    def body(j, carry):
        step(j, False)
        return carry

    lax.fori_loop(0, i, body, 0)
    step(i, True)
    o_ref[...] = _attn_finish(acc_sc[...], l_sc[...], tq, lam_ref[...], g_ref[...]).astype(o_ref.dtype)


def _attn_prompt_shift_kernel(q_ref, k_ref, v_ref, sh_ref, g_ref, lam_ref, o_ref, qs_sc, acc_sc, *, tq):
    i = pl.program_id(1)
    q1, q2 = _stack_maps(q_ref[...])
    qs_sc[0:tq, :] = q1
    qs_sc[tq:2 * tq, :] = q2
    acc_sc[...] = jnp.zeros(acc_sc.shape, F32)
    shift = sh_ref[:, 0:1]
    ones = jnp.ones((tq, LANES), BF16)

    def body(j, carry):
        start = pl.multiple_of(j * tq, tq)
        kb = k_ref[pl.ds(start, tq), :]
        vb1 = jnp.concatenate([v_ref[pl.ds(start, tq), :], ones], axis=1)
        s = lax.dot_general(qs_sc[...], kb, NT_DIMS, preferred_element_type=F32)
        p = jnp.exp2(s - shift).astype(BF16)
        acc_sc[...] += jnp.dot(p, vb1, preferred_element_type=F32)
        return carry

    lax.fori_loop(0, i, body, 0)

    sub = ATTN_DIAG_SUB
    diag0 = pl.multiple_of(i * tq, tq)
    for j in range(tq // sub):
        k0 = pl.multiple_of(diag0 + j * sub, sub)
        kb = k_ref[pl.ds(k0, sub), :]
        vb1 = jnp.concatenate([v_ref[pl.ds(k0, sub), :], ones[:sub]], axis=1)
        for half in range(2):
            rows = slice(half * tq + j * sub, (half + 1) * tq)
            s = lax.dot_general(qs_sc[rows, :], kb, NT_DIMS, preferred_element_type=F32)
            row = lax.broadcasted_iota(jnp.int32, s.shape, 0)
            col = lax.broadcasted_iota(jnp.int32, s.shape, 1)
            s = jnp.where(col // CHUNK <= row // CHUNK, s, NEG_INF)
            p = jnp.exp2(s - shift).astype(BF16)
            acc_sc[rows, :] += jnp.dot(p, vb1, preferred_element_type=F32)

    acc = acc_sc[...]
    o_ref[...] = _attn_finish(acc[:, :A_VDIM], acc[:, A_VDIM:], tq, lam_ref[...], g_ref[...]).astype(o_ref.dtype)


def _attn_prompt_running_max(qn, kb, vb, g, lam, t, n_heads, out_rows):
    tq = ATTN_TILE
    kv_spec = pl.BlockSpec((t, A_VDIM), lambda h, i: (0, h))
    vec = pl.BlockSpec((1, LANES), lambda h, i: (0, 0))
    return pl.pallas_call(
        functools.partial(_attn_prompt_kernel, tq=tq),
        out_shape=jax.ShapeDtypeStruct((out_rows, n_heads * A_VDIM), BF16),
        grid=(n_heads, t // tq),
        in_specs=[pl.BlockSpec((tq, A_VDIM), lambda h, i: (i, h)), kv_spec, kv_spec, vec, vec],
        out_specs=pl.BlockSpec((tq, A_VDIM), lambda h, i: (i, h)),
        scratch_shapes=[pltpu.VMEM((2 * tq, A_VDIM), BF16), pltpu.VMEM((2 * tq, LANES), F32),
                        pltpu.VMEM((2 * tq, LANES), F32), pltpu.VMEM((2 * tq, A_VDIM), F32)],
        compiler_params=_cparams(2),
        name="attn_prompt",
    )(qn, kb, vb, g, lam)


def _attn_prompt_shift(qn, kb, vb, shift, g, lam, t, n_heads, out_rows):
    tq = ATTN_FAST_TILE
    kv_spec = pl.BlockSpec((t, A_VDIM), lambda h, i: (0, h))
    vec = pl.BlockSpec((1, LANES), lambda h, i: (0, 0))
    return pl.pallas_call(
        functools.partial(_attn_prompt_shift_kernel, tq=tq),
        out_shape=jax.ShapeDtypeStruct((out_rows, n_heads * A_VDIM), BF16),
        grid=(n_heads, t // tq),
        in_specs=[pl.BlockSpec((tq, A_VDIM), lambda h, i: (i, h)), kv_spec, kv_spec, vec, vec, vec],
        out_specs=pl.BlockSpec((tq, A_VDIM), lambda h, i: (i, h)),
        scratch_shapes=[pltpu.VMEM((2 * tq, A_VDIM), BF16), pltpu.VMEM((2 * tq, 2 * A_VDIM), F32)],
        compiler_params=_cparams(2),
        name="attn_prompt_shift",
    )(qn, kb, vb, shift, g, lam)


def _attn_prompt(qn, kb, vb, score_bound, g, lam, t, n_heads, out_rows):
    shift = jnp.full((1, LANES), score_bound, F32)
    return lax.cond(
        score_bound <= ATTN_SHIFT_MAX,
        lambda: _attn_prompt_shift(qn, kb, vb, shift, g, lam, t, n_heads, out_rows),
        lambda: _attn_prompt_running_max(qn, kb, vb, g, lam, t, n_heads, out_rows))


def _attn_sample_kernel(q_ref, k_ref, v_ref, g_ref, lam_ref, o_ref, *, tq, n_past, n_keys):
    q1, q2 = _stack_maps(q_ref[...])
    qs = jnp.concatenate([q1, q2], axis=0)
    s = lax.dot_general(qs, k_ref[0], NT_DIMS, preferred_element_type=F32)
    row = lax.broadcasted_iota(jnp.int32, s.shape, 0)
    col = lax.broadcasted_iota(jnp.int32, s.shape, 1)
    visible = (col // CHUNK <= (n_past + row % tq) // CHUNK) & (col < n_keys)
    s = jnp.where(visible, s, NEG_INF)
    p = jnp.exp2(s - jnp.max(s, axis=1, keepdims=True))
    p = jnp.where(col < n_keys, p, 0.0)
    l = jnp.sum(p, axis=1, keepdims=True)
    acc = jnp.dot(p.astype(BF16), v_ref[0], preferred_element_type=F32)
    o_ref[...] = _attn_finish(acc, l, tq, lam_ref[...], g_ref[...]).astype(o_ref.dtype)


def _into_buffers(kernel, n_inputs, bufs, out_indices):
    if bufs is None:
        return kernel, [], (), {}
    n_buf = len(bufs)

    def body(*refs):
        return kernel(*refs[:n_inputs], *refs[n_inputs + n_buf:])

    return (body, [pl.BlockSpec(memory_space=pl.ANY)] * n_buf, tuple(bufs),
            {n_inputs + j: o for j, o in enumerate(out_indices)})


def _into_buffer(kernel, n_inputs, out_buf):
    return _into_buffers(kernel, n_inputs, None if out_buf is None else (out_buf,), (0,))


def _attn_sample(qn, k_all, v_all, g, lam, out_buf, row0, n_batch, tq, n_heads, n_past, n_keys):
    kp = k_all.shape[1]
    q_blk0 = row0 // tq
    kv_spec = pl.BlockSpec((1, kp, A_VDIM), lambda b, h: (b, 0, h))
    vec = pl.BlockSpec((1, LANES), lambda b, h: (0, 0))
    row_spec = pl.BlockSpec((tq, A_VDIM), lambda b, h: (q_blk0 + b, h))
    body, buf_specs, buf_args, aliases = _into_buffer(
        functools.partial(_attn_sample_kernel, tq=tq, n_past=n_past, n_keys=n_keys), 5, out_buf)
    return pl.pallas_call(
        body,
        out_shape=jax.ShapeDtypeStruct(out_buf.shape, out_buf.dtype),
        grid=(n_batch, n_heads),
        in_specs=[row_spec, kv_spec, kv_spec, vec, vec] + buf_specs,
        out_specs=row_spec,
        input_output_aliases=aliases,
        compiler_params=_cparams(2),
        name="attn_sample",
    )(qn, k_all, v_all, g, lam, *buf_args)


def _hgrn_sum_matrix(L):
    t = np.arange(L)[:, None]
    s = np.arange(L)[None, :]
    mats = [s <= t]
    for lev in range(int(math.log2(L))):
        m = (t >> (lev + 1) << (lev + 1)) + (1 << lev) - 1
        mats.append(np.where(t > m, (s > m) & (s <= t), (s > t) & (s <= m)))
    return jnp.asarray(np.concatenate(mats, axis=0), BF16)


def _hgrn_kernel(q_ref, f_ref, i_ref, g_ref, m_ref, lb_ref, og_ref, s0_ref, o_ref, sfin_ref, s_sc, *, n_heads):
    c = pl.program_id(1)
    L = q_ref.shape[0]
    dk = B_HEAD_DIM

    @pl.when(c == 0)
    def _():
        s_sc[...] = s0_ref[0]

    qf = _silu(q_ref[...])
    lb = lb_ref[...]
    f = lb + (1.0 - lb) * _sigmoid(f_ref[...])
    kf = 1.0 - f
    vv = i_ref[...]
    gate = _silu(g_ref[...])

    lf_hi, lf_lo = _split(jnp.log(f))
    sums = (jnp.dot(m_ref[...], lf_hi, preferred_element_type=F32)
            + jnp.dot(m_ref[...], lf_lo, preferred_element_type=F32))
    b = sums[0:L]

    row = lax.broadcasted_iota(jnp.int32, (L, L), 0)
    col = lax.broadcasted_iota(jnp.int32, (L, L), 1)
    roww = lax.broadcasted_iota(jnp.int32, qf.shape, 0)
    atts = [jnp.zeros((L, L), F32) for _ in range(n_heads)]
    for lev in range(int(math.log2(L))):
        e = jnp.exp(sums[(lev + 1) * L:(lev + 2) * L])
        upper = ((roww >> lev) & 1) == 1
        qk = jnp.where(upper, qf * e, 0.0).astype(BF16)
        kk = jnp.where(upper, 0.0, kf * e).astype(BF16)
        same_block = (row >> (lev + 1)) == (col >> (lev + 1))
        for h in range(n_heads):
            sl = slice(h * dk, (h + 1) * dk)
            a = lax.dot_general(qk[:, sl], kk[:, sl], NT_DIMS, preferred_element_type=F32)
            atts[h] = atts[h] + jnp.where(same_block, a, 0.0)

    b_last = b[L - 1:L, :]
    qe = (qf * jnp.exp(b)).astype(BF16)
    kd = (kf * jnp.exp(b_last - b)).astype(BF16)
    vb = vv.astype(BF16)
    e_last = jnp.exp(b_last)
    eye = (lax.broadcasted_iota(jnp.int32, (dk, dk), 0) == lax.broadcasted_iota(jnp.int32, (dk, dk), 1))
    for h in range(n_heads):
        sl = slice(h * dk, (h + 1) * dk)
        att = atts[h] + jnp.where(row == col, jnp.sum(qf[:, sl] * kf[:, sl], axis=1, keepdims=True), 0.0)
        s_prev = s_sc[h]
        o = (jnp.dot(att.astype(BF16), vb[:, sl], preferred_element_type=F32)
             + jnp.dot(qe[:, sl], s_prev.astype(BF16), preferred_element_type=F32))
        decay = jnp.where(eye, jnp.broadcast_to(e_last[:, sl], (dk, dk)), 0.0)
        s_sc[h] = _dot3(decay, s_prev) + lax.dot_general(kd[:, sl], vb[:, sl], TN_DIMS, preferred_element_type=F32)
        o_ref[:, sl] = (_rms(o) * og_ref[...] * gate[:, sl]).astype(o_ref.dtype)
    sfin_ref[0] = s_sc[...]


def _hgrn(proj, lb, og, s0, out_buf, col0, row_blk0, n_batch, n_chunks, chunk, w_b):
    n_heads = w_b // B_HEAD_DIM
    cb = col0 // w_b
    sum_mat = _hgrn_sum_matrix(chunk)
    blk = lambda k: pl.BlockSpec((chunk, w_b), lambda b, c, k=k: (row_blk0 + b * n_chunks + c, cb + k))
    vec = lambda n: pl.BlockSpec((1, n), lambda b, c: (0, 0))
    st = pl.BlockSpec((1, n_heads, B_HEAD_DIM, B_HEAD_DIM), lambda b, c: (b, 0, 0, 0))
    body, buf_specs, buf_args, aliases = _into_buffer(functools.partial(_hgrn_kernel, n_heads=n_heads), 8, out_buf)
    return pl.pallas_call(
        body,
        out_shape=(jax.ShapeDtypeStruct((proj.shape[0], w_b), BF16), jax.ShapeDtypeStruct(s0.shape, F32)),
        grid=(n_batch, n_chunks),
        in_specs=[blk(0), blk(1), blk(2), blk(3), pl.BlockSpec(sum_mat.shape, lambda b, c: (0, 0)),
                  vec(w_b), vec(B_HEAD_DIM), st] + buf_specs,
        out_specs=(pl.BlockSpec((chunk, w_b), lambda b, c: (row_blk0 + b * n_chunks + c, 0)), st),
        scratch_shapes=[pltpu.VMEM((n_heads, B_HEAD_DIM, B_HEAD_DIM), F32)],
        input_output_aliases=aliases,
        compiler_params=_cparams(2),
        name="hgrn2",
    )(proj, proj, proj, proj, sum_mat, lb, og, s0, *buf_args)


def _gelu_tanh(x):
    return 0.5 * x * (1.0 + jnp.tanh(math.sqrt(2.0 / math.pi) * (x + 0.044715 * x * x * x)))


def _s5_kernel(u_ref, bre_ref, bim_ref, cre_ref, cim_ref, d_ref, tab_ref, x0re_ref, x0im_ref,
               glu_ref, og_ref, o_ref, fre_ref, fim_ref, xre_sc, xim_sc, cr_sc, ci_sc):
    t = pl.program_id(1)
    rows, w_c = u_ref.shape
    n_state = xre_sc.shape[1]
    n_blk = w_c // LANES
    sb = n_state // n_blk

    @pl.when(t == 0)
    def _():
        cr_sc[...] = jnp.broadcast_to(x0re_ref[0], cr_sc.shape)
        ci_sc[...] = jnp.broadcast_to(x0im_ref[0], ci_sc.shape)

    u = u_ref[...]
    for j in range(n_blk):
        uj = u[:, j * LANES:(j + 1) * LANES]
        xre_sc[:, j * sb:(j + 1) * sb] = _dot2(uj, bre_ref[j])
        xim_sc[:, j * sb:(j + 1) * sb] = _dot2(uj, bim_ref[j])

    half = n_state // 2

    def body(r, carry):
        r0 = pl.multiple_of(r * SUBLANES, SUBLANES)
        for hs in range(2):
            cs = slice(hs * half, (hs + 1) * half)
            xr = xre_sc[pl.ds(r0, SUBLANES), cs]
            xi = xim_sc[pl.ds(r0, SUBLANES), cs]
            for lev, shift in enumerate((1, 2, 4)):
                ar = tab_ref[2 * lev, :, cs]
                ai = tab_ref[2 * lev + 1, :, cs]
                sr = pltpu.roll(xr, shift, 0)
                si = pltpu.roll(xi, shift, 0)
                xr, xi = xr + (ar * sr - ai * si), xi + (ar * si + ai * sr)
            pr = tab_ref[6, :, cs]
            pi = tab_ref[7, :, cs]
            cr = cr_sc[:, cs]
            ci = ci_sc[:, cs]
            xr, xi = xr + (pr * cr - pi * ci), xi + (pr * ci + pi * cr)
            xre_sc[pl.ds(r0, SUBLANES), cs] = xr
            xim_sc[pl.ds(r0, SUBLANES), cs] = xi
            cr_sc[:, cs] = jnp.broadcast_to(xr[SUBLANES - 1:SUBLANES, :], (SUBLANES, half))
            ci_sc[:, cs] = jnp.broadcast_to(xi[SUBLANES - 1:SUBLANES, :], (SUBLANES, half))
        return carry

    lax.fori_loop(0, rows // SUBLANES, body, 0)

    ys = []
    for j in range(n_blk):
        ys.append(jnp.dot(xre_sc[:, j * sb:(j + 1) * sb].astype(BF16), cre_ref[j], preferred_element_type=F32)
                  - jnp.dot(xim_sc[:, j * sb:(j + 1) * sb].astype(BF16), cim_ref[j], preferred_element_type=F32))
    yc = _gelu_tanh(jnp.concatenate(ys, axis=1) + d_ref[...] * u)
    gl = jnp.dot(yc.astype(BF16), glu_ref[...], preferred_element_type=F32)
    v = gl[:, :w_c] * _sigmoid(gl[:, w_c:])
    o_ref[...] = (_rms(v) * og_ref[...]).astype(o_ref.dtype)
    fre_ref[0] = cr_sc[0:1, :]
    fim_ref[0] = ci_sc[0:1, :]


def _s5(proj, wts, x0re, x0im, out_buf, col0, row_blk0, n_batch, n_tiles, tile, w_c):
    bre, bim, cre, cim, dvec, tab, glu, og = wts
    n_state = tab.shape[2]
    cb = col0 // w_c
    full = lambda a: pl.BlockSpec(a.shape, lambda b, t, nd=a.ndim: (0,) * nd)
    st = pl.BlockSpec((1, 1, n_state), lambda b, t: (b, 0, 0))
    body, buf_specs, buf_args, aliases = _into_buffer(_s5_kernel, 11, out_buf)
    return pl.pallas_call(
        body,
        out_shape=(jax.ShapeDtypeStruct((proj.shape[0], w_c), BF16),
                   jax.ShapeDtypeStruct((n_batch, 1, n_state), F32),
                   jax.ShapeDtypeStruct((n_batch, 1, n_state), F32)),
        grid=(n_batch, n_tiles),
        in_specs=[pl.BlockSpec((tile, w_c), lambda b, t: (row_blk0 + b * n_tiles + t, cb)),
                  full(bre), full(bim), full(cre), full(cim), full(dvec), full(tab), st, st,
                  full(glu), full(og)] + buf_specs,
        out_specs=(pl.BlockSpec((tile, w_c), lambda b, t: (row_blk0 + b * n_tiles + t, 0)), st, st),
        scratch_shapes=[pltpu.VMEM((tile, n_state), F32), pltpu.VMEM((tile, n_state), F32),
                        pltpu.VMEM((SUBLANES, n_state), F32), pltpu.VMEM((SUBLANES, n_state), F32)],
        input_output_aliases=aliases,
        compiler_params=_cparams(2),
        name="s5",
    )(proj, bre, bim, cre, cim, dvec, tab, x0re, x0im, glu, og, *buf_args)


def _s5_weights(a_re, a_im, b_re, b_im, c_re, c_im, d, log_dt, glu_w, out_g):
    g, n = a_re.shape
    p = b_re.shape[2]
    lam_re = jnp.minimum(a_re.astype(F32), A_RE_MAX)
    lam_im = a_im.astype(F32)
    dt = jnp.exp(log_dt.astype(F32))[:, None]
    mag = jnp.exp(lam_re * dt)
    abar_re = mag * jnp.cos(lam_im * dt)
    abar_im = mag * jnp.sin(lam_im * dt)
    den = lam_re * lam_re + lam_im * lam_im
    z_re = abar_re - 1.0
    coef_re = (z_re * lam_re + abar_im * lam_im) / den
    coef_im = (abar_im * lam_re - z_re * lam_im) / den
    b_re = b_re.astype(F32)
    b_im = b_im.astype(F32)
    bbar_re = coef_re[..., None] * b_re - coef_im[..., None] * b_im
    bbar_im = coef_re[..., None] * b_im + coef_im[..., None] * b_re

    gpb = LANES // p
    n_blk = g // gpb
    eye = jnp.eye(gpb, dtype=F32)

    def blockdiag_in(bb):
        x = bb.reshape(n_blk, gpb, n, p)
        return jnp.einsum('jgnp,gh->jgphn', x, eye).reshape(n_blk, gpb * p, gpb * n)

    def blockdiag_out(cc):
        x = cc.astype(F32).reshape(n_blk, gpb, p, n)
        return jnp.einsum('jgpn,gh->jgnhp', x, eye).reshape(n_blk, gpb * n, gpb * p)

    pw_re, pw_im = [abar_re.reshape(-1)], [abar_im.reshape(-1)]
    for _ in range(SUBLANES - 1):
        pr, pi = pw_re[-1], pw_im[-1]
        pw_re.append(pr * pw_re[0] - pi * pw_im[0])
        pw_im.append(pr * pw_im[0] + pi * pw_re[0])
    rows = jnp.arange(SUBLANES)[:, None]
    tabs = []
    for k in (1, 2, 4):
        keep = rows >= k
        tabs.append(jnp.where(keep, pw_re[k - 1][None, :], 0.0))
        tabs.append(jnp.where(keep, pw_im[k - 1][None, :], 0.0))
    tabs.append(jnp.stack(pw_re))
    tabs.append(jnp.stack(pw_im))
    tab = jnp.stack(tabs)
    return (blockdiag_in(bbar_re).astype(BF16), blockdiag_in(bbar_im).astype(BF16),
            blockdiag_out(c_re).astype(BF16), blockdiag_out(c_im).astype(BF16),
            d.astype(F32).reshape(1, g * p), tab, glu_w.astype(BF16), out_g.astype(F32).reshape(1, -1))


def _take_rows(a, idx):
    return a.at[idx].get(mode="promise_in_bounds")


def _moe_plan(route, n_experts, m_pad):
    n_tok = route.shape[0]
    n = TOP_K * n_tok
    tm = MOE_ROW_TILE
    e_flat = jnp.concatenate([route[:, 0], route[:, 1]]).astype(jnp.int32)
    onehot = (e_flat[:, None] == jnp.arange(n_experts, dtype=jnp.int32)[None, :]).astype(jnp.int32)
    csum = jnp.cumsum(onehot, axis=0)
    counts = csum[-1]
    rank = jnp.sum(csum * onehot, axis=1) - 1
    padded = (counts + tm - 1) // tm * tm
    ends_pad = jnp.cumsum(padded)
    start_pad = ends_pad - padded
    start = jnp.cumsum(counts) - counts
    pos = jnp.sum(onehot * start_pad[None, :], axis=1) + rank
    order = jnp.argsort(e_flat, stable=True)
    n_tiles = m_pad // tm
    tile_start = jnp.arange(n_tiles, dtype=jnp.int32) * tm
    tile_expert = jnp.sum((tile_start[:, None] >= ends_pad[None, :]).astype(jnp.int32), axis=1)
    tile_expert = jnp.minimum(tile_expert, jnp.max(e_flat))
    k = jnp.arange(m_pad, dtype=jnp.int32) - jnp.repeat(start_pad[tile_expert], tm)
    valid = k < jnp.repeat(counts[tile_expert], tm)
    slot = jnp.clip(jnp.repeat(start[tile_expert], tm) + k, 0, n - 1)
    row_token = jnp.where(valid, _take_rows(order, slot) % n_tok, 0)
    n_valid = (ends_pad[-1] // tm).astype(jnp.int32)
    return row_token, pos[:n_tok], pos[n_tok:], tile_expert, n_valid.reshape(1)


def _cast_pad_kernel(w_ref, o_ref):
    n = w_ref.shape[1]
    o_ref[:, :n] = w_ref[...].astype(o_ref.dtype)
    if o_ref.shape[1] > n:
        o_ref[:, n:] = jnp.zeros((o_ref.shape[0], o_ref.shape[1] - n), o_ref.dtype)


def _cast_pad_cols(w, n_pad):
    lead, n = w.shape[:-1], w.shape[-1]
    rows = math.prod(lead)
    tr = max(r for r in (1024, 512, 256, 128, 64, 32, 16)
             if rows % r == 0 and (r * n * 4 <= CAST_BLOCK_BYTES or r == 16))
    out = pl.pallas_call(
        _cast_pad_kernel,
        out_shape=jax.ShapeDtypeStruct((rows, n_pad), BF16),
        grid=(rows // tr,),
        in_specs=[pl.BlockSpec((tr, n), lambda i: (i, 0))],
        out_specs=pl.BlockSpec((tr, n_pad), lambda i: (i, 0)),
        compiler_params=_cparams(1),
        name="cast_pad",
    )(w.reshape(rows, n))
    return out.reshape(lead + (n_pad,))


def kernel(x_prompt, x_sample, c_prompt, c_sample, cache_k, cache_v, state_hgrn, state_ssm_re, state_ssm_im, w_ada, b_ada, norm1_g, norm2_g, w_in, w_out, q_norm_g, k_norm_g, lambda_qk, attn_out_g, hgrn_lower_bounds, hgrn_out_g, ssm_a_re, ssm_a_im, ssm_b_re, ssm_b_im, ssm_c_re, ssm_c_im, ssm_d, ssm_log_dt, ssm_glu_w, ssm_out_g, ffn_w1, ffn_w3, ffn_w2, moe_router, moe_w1, moe_w3, moe_w2):
    bp, t, d = x_prompt.shape
    bs, ts, _ = x_sample.shape
    depth = w_in.shape[0]
    n_past = cache_k.shape[2]
    n_heads = cache_k.shape[3]
    w_a = n_heads * A_VDIM
    w_b = hgrn_lower_bounds.shape[1]
    w_c = ssm_out_g.shape[1]
    n_state = ssm_a_re.shape[1] * ssm_a_re.shape[2]
    n_sample = bs * ts
    assert bp == 1 and t % ROW_TILE == 0 and t % ATTN_TILE == 0 and t % ATTN_FAST_TILE == 0
    assert t % S5_TILE == 0 and t % HGRN_TILE == 0
    assert n_sample % ROW_TILE == 0 and ROW_TILE % ts == 0 and ts <= CHUNK and ts % 16 == 0
    assert (ts & (ts - 1)) == 0 and w_a == 2 * w_b and w_b == w_c
    n_pt = t // ROW_TILE
    m_tot = t + n_sample

    x = jnp.concatenate([x_prompt.reshape(t, d), x_sample.reshape(n_sample, d)], axis=0)

    c_rows = jnp.concatenate([c_prompt, c_sample], axis=0)
    c_rows = jnp.pad(c_rows, ((0, _round_up(1 + bs, 16) - (1 + bs)), (0, 0)))
    mod = _modulation(c_rows.astype(F32), w_ada, b_ada)

    def mod_blocks(l, k):
        m = mod[l, :, k * d:(k + 1) * d]
        return jnp.concatenate([jnp.broadcast_to(m[0:1], (ROW_TILE, d)),
                                jnp.repeat(m[1:1 + bs], ts, axis=0)], axis=0)

    lb_p = jax.nn.softmax(hgrn_lower_bounds.astype(F32), axis=0)
    lower_bounds = jnp.cumsum(lb_p, axis=0) - lb_p[0]

    zero_hgrn = jnp.zeros((bp,) + state_hgrn.shape[2:], F32)
    zero_ssm = jnp.zeros((bp, 1, n_state), F32)
    n_keys = n_past + ts
    keys_pad = _round_up(n_keys, LANES)

    sp, rep, imp = [], [], []
    ss, res, ims = [], [], []
    kv_out = None
    y_split = None
    for l in range(depth):
        sh1, sc1, g1, sh2, sc2, g2 = [mod_blocks(l, k) for k in range(6)]
        h = _norm_mod(x, norm1_g[l].reshape(1, d), sc1, sh1, n_pt)
        proj = _matmul(h, _cast_pad_cols(w_in[l], w_in.shape[2]))

        lam_init = 0.8 - 0.6 * math.exp(-0.3 * l)
        lq = lambda_qk[l].astype(F32)
        lam = jnp.exp(jnp.sum(lq[0] * lq[1])) - jnp.exp(jnp.sum(lq[2] * lq[3])) + lam_init
        lam_vec = jnp.full((1, LANES), lam, F32)
        og_a = (attn_out_g[l].astype(F32) * (1.0 - lam_init)).reshape(1, A_VDIM)
        gq = jnp.tile(q_norm_g[l].astype(F32), 2).reshape(1, LANES)
        gk = jnp.tile(k_norm_g[l].astype(F32), 2).reshape(1, LANES)
        qn, kb, vb, kv_out = _qknorm(proj, gq, gk, w_a, l, depth, n_pt, kv_out)
        score_bound = 1.02 * A_HEAD_DIM * Q_SCALE * jnp.max(jnp.abs(gq)) * jnp.max(jnp.abs(gk))
        oa = _attn_prompt(qn, kb, vb, score_bound, og_a, lam_vec, t, n_heads, m_tot)
        pad = jnp.zeros((bs, keys_pad - n_keys, w_a), BF16)
        k_all = jnp.concatenate([cache_k[l].reshape(bs, n_past, w_a).astype(BF16),
                                 kb[t:].reshape(bs, ts, w_a), pad], axis=1)
        v_all = jnp.concatenate([cache_v[l].reshape(bs, n_past, w_a).astype(BF16),
                                 vb[t:].reshape(bs, ts, w_a), pad], axis=1)
        oa = _attn_sample(qn, k_all, v_all, og_a, lam_vec, oa, t, bs, ts, n_heads, n_past, n_keys)

        lb = lower_bounds[l].reshape(1, w_b)
        og_b = hgrn_out_g[l].astype(F32).reshape(1, B_HEAD_DIM)
        col_b = 3 * w_a
        ob, s_p = _hgrn(proj, lb, og_b, zero_hgrn, None, col_b, 0, bp, t // HGRN_TILE, HGRN_TILE, w_b)
        ob, s_s = _hgrn(proj, lb, og_b, state_hgrn[l].astype(F32), ob, col_b, t // ts, bs, 1, ts, w_b)

        wts = _s5_weights(ssm_a_re[l], ssm_a_im[l], ssm_b_re[l], ssm_b_im[l], ssm_c_re[l], ssm_c_im[l],
                          ssm_d[l], ssm_log_dt[l], ssm_glu_w[l], ssm_out_g[l])
        col_c = 3 * w_a + 4 * w_b
        oc, re_p, im_p = _s5(proj, wts, zero_ssm, zero_ssm, None, col_c, 0, bp, t // S5_TILE, S5_TILE, w_c)
        oc, re_s, im_s = _s5(proj, wts, state_ssm_re[l].astype(F32).reshape(bs, 1, n_state),
                             state_ssm_im[l].astype(F32).reshape(bs, 1, n_state),
                             oc, col_c, t // ts, bs, 1, ts, w_c)

        x = _matmul_residual((oa, ob, oc), _cast_pad_cols(w_out[l], d), x, g1, n_pt, tn_target=1024)

        if l % 2 == 0:
            j = l // 2
            dff = ffn_w1.shape[2]
            dff_pad = _round_up(dff, 512)
            h2 = _norm_mod(x, norm2_g[l].reshape(1, d), sc2, sh2, n_pt)
            a = _swiglu_up(h2, _cast_pad_cols(ffn_w1[j], dff_pad), _cast_pad_cols(ffn_w3[j], dff_pad))
            x = _matmul_residual(a, _cast_pad_cols(ffn_w2[j], d), x, g2, n_pt)
        else:
            j = l // 2
            n_experts = moe_router.shape[2]
            h2, route = _norm_mod(x, norm2_g[l].reshape(1, d), sc2, sh2, n_pt, w_router=moe_router[j])
            m_pad = TOP_K * m_tot + n_experts * MOE_ROW_TILE
            row_token, pos0, pos1, tile_expert, n_valid = _moe_plan(route, n_experts, m_pad)
            up = _grouped_swiglu_up(_take_rows(h2, row_token), moe_w1[j], moe_w3[j], tile_expert, n_valid)
            down = _grouped_down(up, moe_w2[j], tile_expert, n_valid)
            r0, r1 = _take_rows(down, pos0), _take_rows(down, pos1)
            if l == depth - 1:
                y_split = _moe_combine(x, r0, r1, route, g2, n_pt, split=True)
            else:
                x = _moe_combine(x, r0, r1, route, g2, n_pt)

        sp.append(s_p)
        rep.append(re_p.reshape(bp, -1, C_STATE))
        imp.append(im_p.reshape(bp, -1, C_STATE))
        ss.append(s_s)
        res.append(re_s.reshape(bs, -1, C_STATE))
        ims.append(im_s.reshape(bs, -1, C_STATE))

    y_p, y_s = y_split if y_split is not None else (x[:t], x[t:])
    k_p, k_s, v_p, v_s = kv_out
    return (y_p.reshape(bp, t, d), y_s.reshape(bs, ts, d),
            k_p.reshape(depth, bp, t, n_heads, 2, A_HEAD_DIM), v_p.reshape(depth, bp, t, n_heads, A_VDIM),
            jnp.stack(sp), jnp.stack(rep), jnp.stack(imp),
            k_s.reshape(depth, bs, ts, n_heads, 2, A_HEAD_DIM), v_s.reshape(depth, bs, ts, n_heads, A_VDIM),
            jnp.stack(ss), jnp.stack(res), jnp.stack(ims))
```
